```python
import math
import jax, jax.numpy as jnp
from jax import lax
import numpy as np

D_MODEL = 1024
BATCH = 8
SEQ = 2048
DEPTH = 1

GRID_W = 64
CTX_LEN = 256
HEAD_DIM = 64
N_HEADS = 16
N_KV_HEADS = 4
GROUP = N_HEADS // N_KV_HEADS
ROPE_AXIS_DIM = HEAD_DIM // 2
ROPE_THETA = 10000.0
Q_BLOCK = 128
N_FGROUPS = 4
FGROUP_DIM = 128
N_BRANCH = 2
Q_W = N_HEADS * HEAD_DIM
KV_W = N_KV_HEADS * HEAD_DIM
F_W = N_FGROUPS * FGROUP_DIM
K_END = Q_W + KV_W
V_END = K_END + KV_W
F_END = V_END + F_W
IN_W = F_END + N_BRANCH * D_MODEL
D_FF = ((8 * D_MODEL // 3 + 255) // 256) * 256
EPS = 1e-6

kernel_name = 'hybrid_gqa_fnet_dit_layer'


def rms_norm(x, gain):
    xf = x.astype(jnp.float32)
    y = xf * lax.rsqrt(jnp.mean(xf * xf, axis=-1, keepdims=True) + EPS)
    return (y * gain.astype(jnp.float32)).astype(x.dtype)


def modulate(h, shift, scale):
    return h * (1 + scale) + shift


def adaln(cond, w_ada, b_ada):
    m = jax.nn.silu(cond) @ w_ada + b_ada
    return jnp.split(m, 6, axis=-1)


def axial_rope_tables(rows, dtype):
    t_row = jnp.repeat(jnp.arange(rows, dtype=jnp.float32), GRID_W)
    t_col = jnp.tile(jnp.arange(GRID_W, dtype=jnp.float32), rows)
    inv_freq = ROPE_THETA ** (-jnp.arange(0, ROPE_AXIS_DIM, 2, dtype=jnp.float32) / ROPE_AXIS_DIM)
    ang = jnp.concatenate([t_row[:, None] * inv_freq, t_col[:, None] * inv_freq], axis=-1)
    return jnp.cos(ang)[:, None, :].astype(dtype), jnp.sin(ang)[:, None, :].astype(dtype)


def apply_rope(x, cos, sin):
    x1, x2 = jnp.split(x, 2, axis=-1)
    return jnp.concatenate([x1 * cos - x2 * sin, x2 * cos + x1 * sin], axis=-1)


def split_in_proj(z):
    return jnp.split(z, [Q_W, K_END, V_END, F_END], axis=-1)


def to_heads(t, n):
    return t.reshape(t.shape[0], t.shape[1], n, HEAD_DIM)


def blocked_attention(q, k, v):
    B, L = q.shape[0], q.shape[1]
    nb = L // Q_BLOCK
    qb = q.reshape(B, nb, Q_BLOCK, N_KV_HEADS, GROUP, HEAD_DIM).transpose(1, 0, 2, 3, 4, 5)
    scale = HEAD_DIM ** -0.5

    def one_block(q_blk):
        s = jnp.einsum('bqkgd,btkd->bkgqt', q_blk, k).astype(jnp.float32) * scale
        p = jax.nn.softmax(s, axis=-1).astype(v.dtype)
        return jnp.einsum('bkgqt,btkd->bqkgd', p, v)

    o = lax.map(one_block, qb)
    return o.transpose(1, 0, 2, 3, 4, 5).reshape(B, L, Q_W)


def fourier_mix(f):
    B, L = f.shape[0], f.shape[1]
    fg = f.reshape(B, L, N_FGROUPS, FGROUP_DIM).astype(jnp.float32)
    y = jnp.real(jnp.fft.fft2(fg, axes=(1, 3), norm='ortho'))
    return y.reshape(B, L, F_W).astype(f.dtype)


def merge_branches(attn, four, gate_logits, b_gate, w_attn_proj, w_fourier_proj, w_o):
    g_attn, g_four = jnp.split(jax.nn.sigmoid(gate_logits + b_gate), N_BRANCH, axis=-1)
    merged = g_attn * (attn @ w_attn_proj) + g_four * (fourier_mix(four) @ w_fourier_proj)
    return merged @ w_o


def swiglu(h, w_gate_up, w_down):
    gate, up = jnp.split(h @ w_gate_up, 2, axis=-1)
    return (jax.nn.silu(gate) * up) @ w_down


def setup_inputs(seed: int = 0) -> dict:
    key = jax.random.key(seed)
    ks = jax.random.split(key, 18)

    def nrm(k, shape, scale):
        return jax.random.normal(k, shape, jnp.float32) * scale

    return {
        'x': nrm(ks[0], (BATCH, SEQ, D_MODEL), 1.0),
        'c': nrm(ks[1], (BATCH, D_MODEL), 1.0),
        'ctx': nrm(ks[2], (BATCH, CTX_LEN, D_MODEL), 1.0),
        'c_ctx': nrm(ks[3], (D_MODEL,), 1.0),
        'w_ada': nrm(ks[4], (DEPTH, D_MODEL, 6 * D_MODEL), 0.5 * D_MODEL ** -0.5),
        'b_ada': nrm(ks[5], (DEPTH, 6 * D_MODEL), 0.01),
        'g_norm_mix': 1.0 + nrm(ks[6], (DEPTH, D_MODEL), 0.1),
        'g_norm_ffn': 1.0 + nrm(ks[7], (DEPTH, D_MODEL), 0.1),
        'w_in': nrm(ks[8], (DEPTH, D_MODEL, IN_W), D_MODEL ** -0.5),
        'b_gate': nrm(ks[9], (DEPTH, N_BRANCH * D_MODEL), 0.1),
        'g_q': 1.0 + nrm(ks[10], (DEPTH, HEAD_DIM), 0.1),
        'g_k': 1.0 + nrm(ks[11], (DEPTH, HEAD_DIM), 0.1),
        'w_attn_proj': nrm(ks[12], (DEPTH, Q_W, D_MODEL), Q_W ** -0.5),
        'w_fourier_proj': nrm(ks[13], (DEPTH, F_W, D_MODEL), F_W ** -0.5),
        'w_o': nrm(ks[14], (DEPTH, D_MODEL, D_MODEL), D_MODEL ** -0.5),
        'w_gate_up': nrm(ks[15], (DEPTH, D_MODEL, 2 * D_FF), D_MODEL ** -0.5),
        'w_down': nrm(ks[16], (DEPTH, D_FF, D_MODEL), D_FF ** -0.5),
        'g_final': 1.0 + nrm(ks[17], (D_MODEL,), 0.1),
    }


def reference(x, c, ctx, c_ctx, w_ada, b_ada, g_norm_mix, g_norm_ffn, w_in, b_gate, g_q, g_k,
              w_attn_proj, w_fourier_proj, w_o, w_gate_up, w_down, g_final):
    B, S = x.shape[0], x.shape[1]
    rows = S // GRID_W
    cos, sin = axial_rope_tables(rows, x.dtype)
    cond_lat = c[:, None, :]
    cond_ctx = c_ctx[None, None, :]
    for layer in range(DEPTH):
        last = layer == DEPTH - 1
        w_in_l = w_in[layer]
        sh_m, sc_m, gt_m, sh_f, sc_f, gt_f = adaln(cond_lat, w_ada[layer], b_ada[layer])
        csh_m, csc_m, cgt_m, csh_f, csc_f, cgt_f = adaln(cond_ctx, w_ada[layer], b_ada[layer])

        h_c = modulate(rms_norm(ctx, g_norm_mix[layer]), csh_m, csc_m)
        if last:
            kc, vc = jnp.split(h_c @ w_in_l[:, Q_W:V_END], 2, axis=-1)
        else:
            qc, kc, vc, fc, glc = split_in_proj(h_c @ w_in_l)
        kc = rms_norm(to_heads(kc, N_KV_HEADS), g_k[layer])
        vc = to_heads(vc, N_KV_HEADS)

        h_x = modulate(rms_norm(x, g_norm_mix[layer]), sh_m, sc_m)
        qx, kx, vx, fx, glx = split_in_proj(h_x @ w_in_l)
        qx = apply_rope(rms_norm(to_heads(qx, N_HEADS), g_q[layer]), cos, sin)
        kx = apply_rope(rms_norm(to_heads(kx, N_KV_HEADS), g_k[layer]), cos, sin)
        vx = to_heads(vx, N_KV_HEADS)
        k_all = jnp.concatenate([kc, kx], axis=1)
        v_all = jnp.concatenate([vc, vx], axis=1)
        attn_x = blocked_attention(qx, k_all, v_all)
        mix_x = merge_branches(attn_x, fx, glx, b_gate[layer], w_attn_proj[layer],
                               w_fourier_proj[layer], w_o[layer])
        x = x + gt_m * mix_x
        x = x + gt_f * swiglu(modulate(rms_norm(x, g_norm_ffn[layer]), sh_f, sc_f),
                              w_gate_up[layer], w_down[layer])

        if not last:
            qc = rms_norm(to_heads(qc, N_HEADS), g_q[layer])
            attn_c = blocked_attention(qc, kc, vc)
            mix_c = merge_branches(attn_c, fc, glc, b_gate[layer], w_attn_proj[layer],
                                   w_fourier_proj[layer], w_o[layer])
            ctx = ctx + cgt_m * mix_c
            ctx = ctx + cgt_f * swiglu(modulate(rms_norm(ctx, g_norm_ffn[layer]), csh_f, csc_f),
                                       w_gate_up[layer], w_down[layer])
    return rms_norm(x, g_final)
```

```python
import functools

import numpy as np
import jax
import jax.numpy as jnp
from jax import lax
from jax.experimental import pallas as pl
from jax.experimental.pallas import tpu as pltpu

D_MODEL = 1024
GRID_W = 64
HEAD_DIM = 64
N_HEADS = 16
N_KV_HEADS = 4
GROUP = N_HEADS // N_KV_HEADS
ROPE_AXIS_DIM = HEAD_DIM // 2
ROPE_THETA = 10000.0
N_FGROUPS = 4
FGROUP_DIM = 128
N_BRANCH = 2
Q_W = N_HEADS * HEAD_DIM
KV_W = N_KV_HEADS * HEAD_DIM
F_W = N_FGROUPS * FGROUP_DIM
K_END = Q_W + KV_W
V_END = K_END + KV_W
F_END = V_END + F_W
EPS = 1e-6

BF16 = jnp.bfloat16
F32 = jnp.float32

VMEM_LIMIT_BYTES = 56 * 1024 * 1024

COND_ROWS = 16
ONES_ROWS = 16
TM_PROJ = 512
TM_FFN = 256
TQ = 256
TR_FOURIER = 512


def _params(*sem):
    return pltpu.CompilerParams(dimension_semantics=sem, vmem_limit_bytes=VMEM_LIMIT_BYTES)


def _const_spec(shape):
    nd = len(shape)
    return pl.BlockSpec(shape, lambda *_: (0,) * nd)


def _sigmoid(x):
    return 1.0 / (1.0 + jnp.exp(-x))


def _modulated_norm(x, gain, shift, scale):
    ms = jnp.mean(x * x, axis=-1, keepdims=True)
    y = x * lax.rsqrt(ms + EPS) * gain
    return y * (1.0 + scale) + shift


def _head_norm_t(zh, gain):
    ms = jnp.mean(zh * zh, axis=0, keepdims=True)
    return zh * lax.rsqrt(ms + EPS) * gain


def _rope_t(zn, cos, sin):
    half = HEAD_DIM // 2
    x1, x2 = zn[:half], zn[half:]
    return x1 * cos - x2 * sin, x2 * cos + x1 * sin


def _adaln_kernel(cond_ref, w_ref, b_ref, o_ref):
    c = cond_ref[...]
    s = c * _sigmoid(c)
    o_ref[...] = jnp.dot(s, w_ref[...], preferred_element_type=F32,
                         precision=lax.Precision.HIGHEST) + b_ref[...]


def _adaln(cond, w, b):
    n = w.shape[1]
    tn = n // 4
    return pl.pallas_call(
        _adaln_kernel,
        grid=(n // tn,),
        in_specs=[_const_spec((COND_ROWS, D_MODEL)),
                  pl.BlockSpec((D_MODEL, tn), lambda j: (0, j)),
                  pl.BlockSpec((1, tn), lambda j: (0, j))],
        out_specs=pl.BlockSpec((COND_ROWS, tn), lambda j: (0, j)),
        out_shape=jax.ShapeDtypeStruct((COND_ROWS, n), F32),
        compiler_params=_params("parallel"),
        name="adaln",
    )(cond, w, b.reshape(1, n))


def _ctx_kv_kernel(x_ref, shift_ref, scale_ref, gnorm_ref, wkvT_ref, gk_ref, k_ref, vT_ref):
    h = _modulated_norm(x_ref[0], gnorm_ref[...], shift_ref[...], scale_ref[...]).astype(BF16)
    zT = lax.dot_general(wkvT_ref[...], h, (((1,), (1,)), ((), ())), preferred_element_type=F32)
    gk = gk_ref[...]
    kT = jnp.concatenate(
        [_head_norm_t(zT[i * HEAD_DIM:(i + 1) * HEAD_DIM], gk) for i in range(N_KV_HEADS)], axis=0)
    k_ref[0] = kT.T.astype(BF16)
    vT_ref[0] = zT[KV_W:].astype(BF16)


def _ctx_kv(ctx, shift, scale, gnorm, wkvT, gk_b):
    B, C, D = ctx.shape
    return pl.pallas_call(
        _ctx_kv_kernel,
        grid=(B,),
        in_specs=[pl.BlockSpec((1, C, D), lambda b: (b, 0, 0)),
                  _const_spec((1, D)), _const_spec((1, D)), _const_spec((1, D)),
                  _const_spec((2 * KV_W, D)), _const_spec((HEAD_DIM, C))],
        out_specs=[pl.BlockSpec((1, C, KV_W), lambda b: (b, 0, 0)),
                   pl.BlockSpec((1, KV_W, C), lambda b: (b, 0, 0))],
        out_shape=[jax.ShapeDtypeStruct((B, C, KV_W), BF16),
                   jax.ShapeDtypeStruct((B, KV_W, C), BF16)],
        compiler_params=_params("parallel"),
        name="ctx_kv",
    )(ctx, shift, scale, gnorm, wkvT, gk_b)


def _in_proj_kernel(x_ref, shift_ref, scale_ref, gnorm_ref, wqkvT_ref, wf_ref, gq_ref, gk_ref,
                    cos_ref, sin_ref, qT_ref, k_ref, vT_ref, f_ref):
    h = _modulated_norm(x_ref[0], gnorm_ref[...], shift_ref[0], scale_ref[0]).astype(BF16)
    zT = lax.dot_general(wqkvT_ref[...], h, (((1,), (1,)), ((), ())), preferred_element_type=F32)
    cos, sin = cos_ref[...], sin_ref[...]
    gq, gk = gq_ref[...], gk_ref[...]
    half = HEAD_DIM // 2
    for i in range(N_HEADS):
        lo = i * HEAD_DIM
        a, b = _rope_t(_head_norm_t(zT[lo:lo + HEAD_DIM], gq), cos, sin)
        qT_ref[0, lo:lo + half, :] = a.astype(BF16)
        qT_ref[0, lo + half:lo + HEAD_DIM, :] = b.astype(BF16)
    k_parts = []
    for i in range(N_KV_HEADS):
        lo = Q_W + i * HEAD_DIM
        k_parts.extend(_rope_t(_head_norm_t(zT[lo:lo + HEAD_DIM], gk), cos, sin))
    k_ref[0] = jnp.concatenate(k_parts, axis=0).T.astype(BF16)
    vT_ref[0] = zT[K_END:V_END].astype(BF16)
    f_ref[0] = jnp.dot(h, wf_ref[...], preferred_element_type=F32).astype(BF16)


def _in_proj(x, shift, scale, gnorm, wqkvT, wf, gq_b, gk_b, cosT, sinT):
    B, S, D = x.shape
    tm = TM_PROJ
    half = HEAD_DIM // 2
    return pl.pallas_call(
        _in_proj_kernel,
        grid=(B, S // tm),
        in_specs=[pl.BlockSpec((1, tm, D), lambda b, i: (b, i, 0)),
                  pl.BlockSpec((1, 1, D), lambda b, i: (b, 0, 0)),
                  pl.BlockSpec((1, 1, D), lambda b, i: (b, 0, 0)),
                  _const_spec((1, D)),
                  _const_spec((V_END, D)), _const_spec((D, F_W)),
                  _const_spec((HEAD_DIM, tm)), _const_spec((HEAD_DIM, tm)),
                  pl.BlockSpec((half, tm), lambda b, i: (0, i)),
                  pl.BlockSpec((half, tm), lambda b, i: (0, i))],
        out_specs=[pl.BlockSpec((1, Q_W, tm), lambda b, i: (b, 0, i)),
                   pl.BlockSpec((1, tm, KV_W), lambda b, i: (b, i, 0)),
                   pl.BlockSpec((1, KV_W, tm), lambda b, i: (b, 0, i)),
                   pl.BlockSpec((1, tm, F_W), lambda b, i: (b, i, 0))],
        out_shape=[jax.ShapeDtypeStruct((B, Q_W, S), BF16),
                   jax.ShapeDtypeStruct((B, S, KV_W), BF16),
                   jax.ShapeDtypeStruct((B, KV_W, S), BF16),
                   jax.ShapeDtypeStruct((B, S, F_W), BF16)],
        compiler_params=_params("parallel", "parallel"),
        name="in_proj",
    )(x, shift, scale, gnorm, wqkvT, wf, gq_b, gk_b, cosT, sinT)


def _attention_kernel(qT_ref, k_ref, kc_ref, vT_ref, vcT_ref, o_ref):
    g = pl.program_id(1)
    tq = qT_ref.shape[2]
    s_lat_len = k_ref.shape[1]
    s_ctx_len = kc_ref.shape[1]
    in_group = lax.broadcasted_iota(jnp.int32, (KV_W, tq), 0) // HEAD_DIM == g
    k_lat, k_ctx = k_ref[0], kc_ref[0]
    v_lat = jnp.concatenate([vT_ref[0], jnp.ones((ONES_ROWS, s_lat_len), BF16)], axis=0)
    v_ctx = jnp.concatenate([vcT_ref[0], jnp.ones((ONES_ROWS, s_ctx_len), BF16)], axis=0)
    outs = []
    for j in range(GROUP):
        qj = qT_ref[0, j * HEAD_DIM:(j + 1) * HEAD_DIM, :].astype(F32)
        qz = jnp.where(in_group, jnp.concatenate([qj] * N_KV_HEADS, axis=0), 0.0).astype(BF16)
        s_lat = jnp.dot(k_lat, qz, preferred_element_type=F32)
        s_ctx = jnp.dot(k_ctx, qz, preferred_element_type=F32)
        m = jnp.maximum(jnp.max(s_lat, axis=0, keepdims=True), jnp.max(s_ctx, axis=0, keepdims=True))
        p_lat = jnp.exp(s_lat - m).astype(BF16)
        p_ctx = jnp.exp(s_ctx - m).astype(BF16)
        oa = (jnp.dot(v_lat, p_lat, preferred_element_type=F32)
              + jnp.dot(v_ctx, p_ctx, preferred_element_type=F32))
        outs.append(oa[:HEAD_DIM] / oa[HEAD_DIM:HEAD_DIM + 1])
    o_ref[0] = jnp.concatenate(outs, axis=0).T.astype(BF16)


def _attention(qT, k_tok, kc_tok, vT, vcT):
    B, _, S = qT.shape
    C = kc_tok.shape[1]
    gw = GROUP * HEAD_DIM
    return pl.pallas_call(
        _attention_kernel,
        grid=(B, N_KV_HEADS, S // TQ),
        in_specs=[pl.BlockSpec((1, gw, TQ), lambda b, g, i: (b, g, i)),
                  pl.BlockSpec((1, S, KV_W), lambda b, g, i: (b, 0, 0)),
                  pl.BlockSpec((1, C, KV_W), lambda b, g, i: (b, 0, 0)),
                  pl.BlockSpec((1, HEAD_DIM, S), lambda b, g, i: (b, g, 0)),
                  pl.BlockSpec((1, HEAD_DIM, C), lambda b, g, i: (b, g, 0))],
        out_specs=pl.BlockSpec((1, TQ, gw), lambda b, g, i: (b, i, g)),
        out_shape=jax.ShapeDtypeStruct((B, S, Q_W), BF16),
        compiler_params=_params("parallel", "parallel", "parallel"),
        name="attention",
    )(qT, k_tok, kc_tok, vT, vcT)


@functools.lru_cache(maxsize=None)
def _dft_tables(n_pos):
    def cos_sin(n):
        idx = np.arange(n, dtype=np.int64)
        ang = 2.0 * np.pi * ((idx[:, None] * idx[None, :]) % n).astype(np.float64) / n
        return np.cos(ang), np.sin(ang)
    cc, sc = cos_sin(FGROUP_DIM)
    cl, sl = cos_sin(n_pos)
    chan = np.concatenate([cc, sc], axis=1).astype(np.float32)
    pos = np.concatenate([cl, -sl], axis=1).astype(np.float32)
    return chan, pos


def _fourier_kernel(f_ref, chan_ref, pos_ref, o_ref, z_ref, *, scale):
    n_pos = f_ref.shape[1]

    @pl.when(pl.program_id(1) == 0)
    def _():
        for g in range(N_FGROUPS):
            cols = slice(g * FGROUP_DIM, (g + 1) * FGROUP_DIM)
            xcs = jnp.dot(f_ref[0, :, cols], chan_ref[...], preferred_element_type=F32)
            z_ref[0:n_pos, cols] = xcs[:, :FGROUP_DIM].astype(BF16)
            z_ref[n_pos:2 * n_pos, cols] = xcs[:, FGROUP_DIM:].astype(BF16)

    y = jnp.dot(pos_ref[...], z_ref[...], preferred_element_type=F32)
    o_ref[0] = (y * scale).astype(BF16)


def _fourier(f):
    B, S, W = f.shape
    chan, pos = _dft_tables(S)
    tr = TR_FOURIER
    scale = float(1.0 / np.sqrt(S * FGROUP_DIM))
    return pl.pallas_call(
        functools.partial(_fourier_kernel, scale=scale),
        grid=(B, S // tr),
        in_specs=[pl.BlockSpec((1, S, W), lambda b, i: (b, 0, 0)),
                  _const_spec((FGROUP_DIM, 2 * FGROUP_DIM)),
                  pl.BlockSpec((tr, 2 * S), lambda b, i: (i, 0))],
        out_specs=pl.BlockSpec((1, tr, W), lambda b, i: (b, i, 0)),
        out_shape=jax.ShapeDtypeStruct((B, S, W), BF16),
        scratch_shapes=[pltpu.VMEM((2 * S, W), BF16)],
        compiler_params=_params("parallel", "arbitrary"),
        name="fourier",
    )(f, jnp.asarray(chan).astype(BF16), jnp.asarray(pos).astype(BF16))


def _merge_kernel(x_ref, shift_ref, scale_ref, gate_ref, gnorm_ref, wg_ref, bg_ref, attn_ref, four_ref,
                  wap_ref, wfp_ref, wo_ref, o_ref):
    x = x_ref[0]
    h = _modulated_norm(x, gnorm_ref[...], shift_ref[0], scale_ref[0]).astype(BF16)
    gates = _sigmoid(jnp.dot(h, wg_ref[...], preferred_element_type=F32) + bg_ref[...])
    a = jnp.dot(attn_ref[0], wap_ref[...], preferred_element_type=F32)
    fo = jnp.dot(four_ref[0], wfp_ref[...], preferred_element_type=F32)
    merged = gates[:, :D_MODEL] * a + gates[:, D_MODEL:] * fo
    mix = jnp.dot(merged.astype(BF16), wo_ref[...], preferred_element_type=F32)
    o_ref[0] = x + gate_ref[0] * mix


def _merge(x, shift, scale, gate, gnorm, wg, bg, attn, four, wap, wfp, wo):
    B, S, D = x.shape
    tm = TM_PROJ
    vec = pl.BlockSpec((1, 1, D), lambda b, i: (b, 0, 0))
    return pl.pallas_call(
        _merge_kernel,
        grid=(B, S // tm),
        in_specs=[pl.BlockSpec((1, tm, D), lambda b, i: (b, i, 0)), vec, vec, vec,
                  _const_spec((1, D)),
                  _const_spec((D, N_BRANCH * D)), _const_spec((1, N_BRANCH * D)),
                  pl.BlockSpec((1, tm, Q_W), lambda b, i: (b, i, 0)),
                  pl.BlockSpec((1, tm, F_W), lambda b, i: (b, i, 0)),
                  _const_spec((Q_W, D)), _const_spec((F_W, D)), _const_spec((D, D))],
        out_specs=pl.BlockSpec((1, tm, D), lambda b, i: (b, i, 0)),
        out_shape=jax.ShapeDtypeStruct((B, S, D), F32),
        compiler_params=_params("parallel", "parallel"),
        name="merge",
    )(x, shift, scale, gate, gnorm, wg, bg, attn, four, wap, wfp, wo)


def _ffn_kernel(x_ref, shift_ref, scale_ref, gate_ref, gnorm_ref, wgu_ref, wdown_ref, gfinal_ref, o_ref):
    x = x_ref[0]
    d_ff = wdown_ref.shape[0]
    h = _modulated_norm(x, gnorm_ref[...], shift_ref[0], scale_ref[0]).astype(BF16)
    gu = jnp.dot(h, wgu_ref[...], preferred_element_type=F32)
    gate, up = gu[:, :d_ff], gu[:, d_ff:]
    act = (gate * _sigmoid(gate) * up).astype(BF16)
    x2 = x + gate_ref[0] * jnp.dot(act, wdown_ref[...], preferred_element_type=F32)
    ms = jnp.mean(x2 * x2, axis=-1, keepdims=True)
    o_ref[0] = x2 * lax.rsqrt(ms + EPS) * gfinal_ref[...]


def _ffn(x, shift, scale, gate, gnorm, wgu, wdown, gfinal):
    B, S, D = x.shape
    d_ff = wdown.shape[0]
    tm = TM_FFN
    vec = pl.BlockSpec((1, 1, D), lambda b, i: (b, 0, 0))
    return pl.pallas_call(
        _ffn_kernel,
        grid=(B, S // tm),
        in_specs=[pl.BlockSpec((1, tm, D), lambda b, i: (b, i, 0)), vec, vec, vec,
                  _const_spec((1, D)),
                  _const_spec((D, 2 * d_ff)), _const_spec((d_ff, D)), _const_spec((1, D))],
        out_specs=pl.BlockSpec((1, tm, D), lambda b, i: (b, i, 0)),
        out_shape=jax.ShapeDtypeStruct((B, S, D), F32),
        compiler_params=_params("parallel", "parallel"),
        name="ffn",
    )(x, shift, scale, gate, gnorm, wgu, wdown, gfinal)


def _rope_tables_t(seq, dtype):
    rows = seq // GRID_W
    t_row = jnp.repeat(jnp.arange(rows, dtype=F32), GRID_W)
    t_col = jnp.tile(jnp.arange(GRID_W, dtype=F32), rows)
    inv_freq = ROPE_THETA ** (-jnp.arange(0, ROPE_AXIS_DIM, 2, dtype=F32) / ROPE_AXIS_DIM)
    ang = jnp.concatenate([t_row[:, None] * inv_freq, t_col[:, None] * inv_freq], axis=-1)
    return jnp.cos(ang).astype(dtype).T, jnp.sin(ang).astype(dtype).T


def kernel(x, c, ctx, c_ctx, w_ada, b_ada, g_norm_mix, g_norm_ffn, w_in, b_gate, g_q, g_k, w_attn_proj,
           w_fourier_proj, w_o, w_gate_up, w_down, g_final):
    B, S, D = x.shape
    C = ctx.shape[1]
    assert D == D_MODEL and w_in.shape[0] == 1 and S % TM_PROJ == 0 and S % TQ == 0 and B < COND_ROWS
    layer = 0

    cond = jnp.zeros((COND_ROWS, D), F32).at[:B].set(c).at[B].set(c_ctx)
    mods = _adaln(cond, w_ada[layer], b_ada[layer])
    sh_m, sc_m, gt_m, sh_f, sc_f, gt_f = [m[:B, None, :] for m in jnp.split(mods, 6, axis=-1)]
    csh_m, csc_m = mods[B:B + 1, 0:D], mods[B:B + 1, D:2 * D]

    w_in_l = w_in[layer]
    wqkvT = w_in_l[:, :V_END].T.astype(BF16)
    wkvT = w_in_l[:, Q_W:V_END].T.astype(BF16)
    wf = w_in_l[:, V_END:F_END].astype(BF16)
    wg = w_in_l[:, F_END:].astype(BF16)
    gnorm_mix = g_norm_mix[layer].reshape(1, D)
    gnorm_ffn = g_norm_ffn[layer].reshape(1, D)

    gq_col = (g_q[layer] * (HEAD_DIM ** -0.5)).reshape(HEAD_DIM, 1)
    gk_col = g_k[layer].reshape(HEAD_DIM, 1)
    gq_b = jnp.broadcast_to(gq_col, (HEAD_DIM, TM_PROJ))
    gk_b = jnp.broadcast_to(gk_col, (HEAD_DIM, TM_PROJ))
    gk_ctx = jnp.broadcast_to(gk_col, (HEAD_DIM, C))
    cosT, sinT = _rope_tables_t(S, x.dtype)

    kc_tok, vcT = _ctx_kv(ctx, csh_m, csc_m, gnorm_mix, wkvT, gk_ctx)
    qT, k_tok, vT, f = _in_proj(x, sh_m, sc_m, gnorm_mix, wqkvT, wf, gq_b, gk_b, cosT, sinT)
    attn = _attention(qT, k_tok, kc_tok, vT, vcT)
    four = _fourier(f)
    x1 = _merge(x, sh_m, sc_m, gt_m, gnorm_mix, wg, b_gate[layer].reshape(1, -1), attn, four,
                w_attn_proj[layer].astype(BF16), w_fourier_proj[layer].astype(BF16), w_o[layer].astype(BF16))
    return _ffn(x1, sh_f, sc_f, gt_f, gnorm_ffn, w_gate_up[layer].astype(BF16), w_down[layer].astype(BF16),
                g_final.reshape(1, D))
```

```python
import functools

import numpy as np
import jax
import jax.numpy as jnp
from jax import lax
from jax.experimental import pallas as pl
from jax.experimental.pallas import tpu as pltpu

D_MODEL = 1024
GRID_W = 64
HEAD_DIM = 64
N_HEADS = 16
N_KV_HEADS = 4
GROUP = N_HEADS // N_KV_HEADS
ROPE_AXIS_DIM = HEAD_DIM // 2
ROPE_THETA = 10000.0
N_FGROUPS = 4
FGROUP_DIM = 128
N_BRANCH = 2
Q_W = N_HEADS * HEAD_DIM
KV_W = N_KV_HEADS * HEAD_DIM
F_W = N_FGROUPS * FGROUP_DIM
K_END = Q_W + KV_W
V_END = K_END + KV_W
F_END = V_END + F_W
EPS = 1e-6

BF16 = jnp.bfloat16
F32 = jnp.float32

VMEM_LIMIT_BYTES = 56 * 1024 * 1024

COND_ROWS = 16
ONES_ROWS = 16
TM_PROJ = 512
TM_FFN = 256
TQ = 512
KEY_CHUNK = 256
TR_FOURIER = 512


def _params(*sem):
    return pltpu.CompilerParams(dimension_semantics=sem, vmem_limit_bytes=VMEM_LIMIT_BYTES)


def _const_spec(shape):
    nd = len(shape)
    return pl.BlockSpec(shape, lambda *_: (0,) * nd)


def _sigmoid(x):
    return 1.0 / (1.0 + jnp.exp(-x))


def _modulated_norm(x, gain, shift, scale):
    ms = jnp.mean(x * x, axis=-1, keepdims=True)
    y = x * lax.rsqrt(ms + EPS) * gain
    return y * (1.0 + scale) + shift


def _head_norm_t(zh, gain):
    ms = jnp.mean(zh * zh, axis=0, keepdims=True)
    return zh * lax.rsqrt(ms + EPS) * gain


def _rope_t(zn, cos, sin):
    half = HEAD_DIM // 2
    x1, x2 = zn[:half], zn[half:]
    return x1 * cos - x2 * sin, x2 * cos + x1 * sin


def _adaln_kernel(cond_ref, w_ref, b_ref, o_ref):
    c = cond_ref[...]
    s = c * _sigmoid(c)
    o_ref[...] = jnp.dot(s, w_ref[...], preferred_element_type=F32,
                         precision=lax.Precision.HIGHEST) + b_ref[...]


def _adaln(cond, w, b):
    n = w.shape[1]
    tn = n // 4
    return pl.pallas_call(
        _adaln_kernel,
        grid=(n // tn,),
        in_specs=[_const_spec((COND_ROWS, D_MODEL)),
                  pl.BlockSpec((D_MODEL, tn), lambda j: (0, j)),
                  pl.BlockSpec((1, tn), lambda j: (0, j))],
        out_specs=pl.BlockSpec((COND_ROWS, tn), lambda j: (0, j)),
        out_shape=jax.ShapeDtypeStruct((COND_ROWS, n), F32),
        compiler_params=_params("parallel"),
        name="adaln",
    )(cond, w, b.reshape(1, n))


def _in_proj_kernel(x_ref, shift_ref, scale_ref, gnorm_ref, wqkvT_ref, wf_ref, gq_ref, gk_ref,
                    cos_ref, sin_ref, qT_ref, k_ref, vT_ref, f_ref):
    h = _modulated_norm(x_ref[0], gnorm_ref[...], shift_ref[0], scale_ref[0]).astype(BF16)
    zT = lax.dot_general(wqkvT_ref[...], h, (((1,), (1,)), ((), ())), preferred_element_type=F32)
    cos, sin = cos_ref[...], sin_ref[...]
    gq, gk = gq_ref[...], gk_ref[...]
    half = HEAD_DIM // 2
    for i in range(N_HEADS):
        lo = i * HEAD_DIM
        a, b = _rope_t(_head_norm_t(zT[lo:lo + HEAD_DIM], gq), cos, sin)
        qT_ref[0, lo:lo + half, :] = a.astype(BF16)
        qT_ref[0, lo + half:lo + HEAD_DIM, :] = b.astype(BF16)
    k_parts = []
    for i in range(N_KV_HEADS):
        lo = Q_W + i * HEAD_DIM
        k_parts.extend(_rope_t(_head_norm_t(zT[lo:lo + HEAD_DIM], gk), cos, sin))
    k_ref[0] = jnp.concatenate(k_parts, axis=0).T.astype(BF16)
    vT_ref[0] = zT[K_END:V_END].astype(BF16)
    f_ref[0] = jnp.dot(h, wf_ref[...], preferred_element_type=F32).astype(BF16)


def _in_proj(x, shift, scale, gnorm, wqkvT, wf, gq_b, gk_b, cosT, sinT, n_ctx):
    B, S, D = x.shape
    tm = TM_PROJ
    half = HEAD_DIM // 2
    return pl.pallas_call(
        _in_proj_kernel,
        grid=(B, S // tm),
        in_specs=[pl.BlockSpec((1, tm, D), lambda b, i: (b, i, 0)),
                  pl.BlockSpec((1, 1, D), lambda b, i: (b, 0, 0)),
                  pl.BlockSpec((1, 1, D), lambda b, i: (b, 0, 0)),
                  _const_spec((1, D)),
                  _const_spec((V_END, D)), _const_spec((D, F_W)),
                  _const_spec((HEAD_DIM, tm)), _const_spec((HEAD_DIM, tm)),
                  pl.BlockSpec((half, tm), lambda b, i: (0, i)),
                  pl.BlockSpec((half, tm), lambda b, i: (0, i))],
        out_specs=[pl.BlockSpec((1, Q_W, tm), lambda b, i: (b, 0, i)),
                   pl.BlockSpec((1, tm, KV_W), lambda b, i: (b, i, 0)),
                   pl.BlockSpec((1, KV_W, tm), lambda b, i: (b, 0, i)),
                   pl.BlockSpec((1, tm, F_W), lambda b, i: (b, i, 0))],
        out_shape=[jax.ShapeDtypeStruct((B, Q_W, S), BF16),
                   jax.ShapeDtypeStruct((B, S + n_ctx, KV_W), BF16),
                   jax.ShapeDtypeStruct((B, KV_W, S + n_ctx), BF16),
                   jax.ShapeDtypeStruct((B, S, F_W), BF16)],
        compiler_params=_params("parallel", "parallel"),
        name="in_proj",
    )(x, shift, scale, gnorm, wqkvT, wf, gq_b, gk_b, cosT, sinT)


def _ctx_kv_kernel(x_ref, shift_ref, scale_ref, gnorm_ref, wkvT_ref, gk_ref, k_ref, vT_ref):
    h = _modulated_norm(x_ref[0], gnorm_ref[...], shift_ref[...], scale_ref[...]).astype(BF16)
    zT = lax.dot_general(wkvT_ref[...], h, (((1,), (1,)), ((), ())), preferred_element_type=F32)
    gk = gk_ref[...]
    kT = jnp.concatenate(
        [_head_norm_t(zT[i * HEAD_DIM:(i + 1) * HEAD_DIM], gk) for i in range(N_KV_HEADS)], axis=0)
    k_ref[0] = kT.T.astype(BF16)
    vT_ref[0] = zT[KV_W:].astype(BF16)


def _ctx_kv(ctx, shift, scale, gnorm, wkvT, gk_b):
    B, C, D = ctx.shape
    return pl.pallas_call(
        _ctx_kv_kernel,
        grid=(B,),
        in_specs=[pl.BlockSpec((1, C, D), lambda b: (b, 0, 0)),
                  _const_spec((1, D)), _const_spec((1, D)), _const_spec((1, D)),
                  _const_spec((2 * KV_W, D)), _const_spec((HEAD_DIM, C))],
        out_specs=[pl.BlockSpec((1, C, KV_W), lambda b: (b, 0, 0)),
                   pl.BlockSpec((1, KV_W, C), lambda b: (b, 0, 0))],
        out_shape=[jax.ShapeDtypeStruct((B, C, KV_W), BF16),
                   jax.ShapeDtypeStruct((B, KV_W, C), BF16)],
        compiler_params=_params("parallel"),
        name="ctx_kv",
    )(ctx, shift, scale, gnorm, wkvT, gk_b)


def _attention_kernel(qT_ref, k_ref, vT_ref, o_ref, vaug_ref, qz_ref, s0_ref, s1_ref, m0_ref, m1_ref, oT_ref):
    qi = pl.program_id(1)
    tq = qT_ref.shape[2]
    n_chunks = k_ref.shape[1] // KEY_CHUNK

    @pl.when(qi == 0)
    def _():
        ones = jnp.ones((ONES_ROWS, KEY_CHUNK), BF16)
        for g in range(N_KV_HEADS):
            for c in range(n_chunks):
                vaug_ref[g, c, :HEAD_DIM, :] = vT_ref[0, g * HEAD_DIM:(g + 1) * HEAD_DIM,
                                                      c * KEY_CHUNK:(c + 1) * KEY_CHUNK]
                vaug_ref[g, c, HEAD_DIM:, :] = ones

    def head_rows(t):
        return slice(t * HEAD_DIM, (t + 1) * HEAD_DIM)

    def load_qz(t):
        qz_ref[...] = jnp.zeros_like(qz_ref)
        qz_ref[head_rows(t // GROUP), :] = qT_ref[0, head_rows(t), :]

    def step(t_b, sb, sa):
        if sa is not None:
            load_qz(t_b + 1)
        m8, oa = jnp.full((8, tq), -jnp.inf, F32), None
        for c in range(n_chunks):
            rows = slice(c * KEY_CHUNK, (c + 1) * KEY_CHUNK)
            if sb is not None:
                p = jnp.exp(sb[0][rows, :] - sb[1][0:1, :]).astype(BF16)
                d = jnp.dot(vaug_ref[t_b // GROUP, c], p, preferred_element_type=F32)
                oa = d if oa is None else oa + d
            if sa is not None:
                s = jnp.dot(k_ref[0, rows, :], qz_ref[...], preferred_element_type=F32)
                sa[0][rows, :] = s
                m8 = jnp.maximum(m8, jnp.max(s.reshape(KEY_CHUNK // 8, 8, tq), axis=0))
        if sa is not None:
            sa[1][...] = jnp.broadcast_to(jnp.max(m8, axis=0, keepdims=True), (8, tq))
        if sb is not None:
            oT_ref[head_rows(t_b), :] = oa[:HEAD_DIM] / oa[HEAD_DIM:HEAD_DIM + 1]

    buf0, buf1 = (s0_ref, m0_ref), (s1_ref, m1_ref)
    step(-1, None, buf0)

    bufs = (buf0, buf1)
    for t in range(N_HEADS - 1):
        step(t, bufs[t % 2], bufs[(t + 1) % 2])
    step(N_HEADS - 1, bufs[(N_HEADS - 1) % 2], None)
    o_ref[0] = oT_ref[...].T.astype(BF16)


def _attention(qT, k_all, vT_all):
    B, _, S = qT.shape
    T = k_all.shape[1]
    assert T % KEY_CHUNK == 0
    n_chunks = T // KEY_CHUNK
    return pl.pallas_call(
        _attention_kernel,
        grid=(B, S // TQ),
        in_specs=[pl.BlockSpec((1, Q_W, TQ), lambda b, i: (b, 0, i)),
                  pl.BlockSpec((1, T, KV_W), lambda b, i: (b, 0, 0)),
                  pl.BlockSpec((1, KV_W, T), lambda b, i: (b, 0, 0))],
        out_specs=pl.BlockSpec((1, TQ, Q_W), lambda b, i: (b, i, 0)),
        out_shape=jax.ShapeDtypeStruct((B, S, Q_W), BF16),
        scratch_shapes=[pltpu.VMEM((N_KV_HEADS, n_chunks, HEAD_DIM + ONES_ROWS, KEY_CHUNK), BF16),
                        pltpu.VMEM((KV_W, TQ), BF16),
                        pltpu.VMEM((T, TQ), F32), pltpu.VMEM((T, TQ), F32),
                        pltpu.VMEM((8, TQ), F32), pltpu.VMEM((8, TQ), F32),
                        pltpu.VMEM((Q_W, TQ), F32)],
        compiler_params=_params("parallel", "arbitrary"),
        name="attention",
    )(qT, k_all, vT_all)


@functools.lru_cache(maxsize=None)
def _dft_tables(n_pos):
    def cos_sin(n):
        idx = np.arange(n, dtype=np.int64)
        ang = 2.0 * np.pi * ((idx[:, None] * idx[None, :]) % n).astype(np.float64) / n
        return np.cos(ang), np.sin(ang)
    cc, sc = cos_sin(FGROUP_DIM)
    cl, sl = cos_sin(n_pos)
    chan = np.concatenate([cc, sc], axis=1).astype(np.float32)
    pos = np.concatenate([cl, -sl], axis=1).astype(np.float32)
    return chan, pos


def _fourier_kernel(f_ref, chan_ref, pos_ref, o_ref, z_ref, *, scale):
    n_pos = f_ref.shape[1]

    @pl.when(pl.program_id(1) == 0)
    def _():
        for g in range(N_FGROUPS):
            cols = slice(g * FGROUP_DIM, (g + 1) * FGROUP_DIM)
            xcs = jnp.dot(f_ref[0, :, cols], chan_ref[...], preferred_element_type=F32)
            z_ref[0:n_pos, cols] = xcs[:, :FGROUP_DIM].astype(BF16)
            z_ref[n_pos:2 * n_pos, cols] = xcs[:, FGROUP_DIM:].astype(BF16)

    y = jnp.dot(pos_ref[...], z_ref[...], preferred_element_type=F32)
    o_ref[0] = (y * scale).astype(BF16)


def _fourier(f):
    B, S, W = f.shape
    chan, pos = _dft_tables(S)
    tr = TR_FOURIER
    scale = float(1.0 / np.sqrt(S * FGROUP_DIM))
    return pl.pallas_call(
        functools.partial(_fourier_kernel, scale=scale),
        grid=(B, S // tr),
        in_specs=[pl.BlockSpec((1, S, W), lambda b, i: (b, 0, 0)),
                  _const_spec((FGROUP_DIM, 2 * FGROUP_DIM)),
                  pl.BlockSpec((tr, 2 * S), lambda b, i: (i, 0))],
        out_specs=pl.BlockSpec((1, tr, W), lambda b, i: (b, i, 0)),
        out_shape=jax.ShapeDtypeStruct((B, S, W), BF16),
        scratch_shapes=[pltpu.VMEM((2 * S, W), BF16)],
        compiler_params=_params("parallel", "arbitrary"),
        name="fourier",
    )(f, jnp.asarray(chan).astype(BF16), jnp.asarray(pos).astype(BF16))


def _merge_kernel(x_ref, shift_ref, scale_ref, gate_ref, gnorm_ref, wg_ref, bg_ref, attn_ref, four_ref,
                  wap_ref, wfp_ref, wo_ref, o_ref):
    x = x_ref[0]
    h = _modulated_norm(x, gnorm_ref[...], shift_ref[0], scale_ref[0]).astype(BF16)
    gates = _sigmoid(jnp.dot(h, wg_ref[...], preferred_element_type=F32) + bg_ref[...])
    a = jnp.dot(attn_ref[0], wap_ref[...], preferred_element_type=F32)
    fo = jnp.dot(four_ref[0], wfp_ref[...], preferred_element_type=F32)
    merged = gates[:, :D_MODEL] * a + gates[:, D_MODEL:] * fo
    mix = jnp.dot(merged.astype(BF16), wo_ref[...], preferred_element_type=F32)
    o_ref[0] = x + gate_ref[0] * mix


def _merge(x, shift, scale, gate, gnorm, wg, bg, attn, four, wap, wfp, wo):
    B, S, D = x.shape
    tm = TM_PROJ
    vec = pl.BlockSpec((1, 1, D), lambda b, i: (b, 0, 0))
    return pl.pallas_call(
        _merge_kernel,
        grid=(B, S // tm),
        in_specs=[pl.BlockSpec((1, tm, D), lambda b, i: (b, i, 0)), vec, vec, vec,
                  _const_spec((1, D)),
                  _const_spec((D, N_BRANCH * D)), _const_spec((1, N_BRANCH * D)),
                  pl.BlockSpec((1, tm, Q_W), lambda b, i: (b, i, 0)),
                  pl.BlockSpec((1, tm, F_W), lambda b, i: (b, i, 0)),
                  _const_spec((Q_W, D)), _const_spec((F_W, D)), _const_spec((D, D))],
        out_specs=pl.BlockSpec((1, tm, D), lambda b, i: (b, i, 0)),
        out_shape=jax.ShapeDtypeStruct((B, S, D), F32),
        compiler_params=_params("parallel", "parallel"),
        name="merge",
    )(x, shift, scale, gate, gnorm, wg, bg, attn, four, wap, wfp, wo)


def _ffn_kernel(x_ref, shift_ref, scale_ref, gate_ref, gnorm_ref, wgu_ref, wdown_ref, gfinal_ref, o_ref):
    x = x_ref[0]
    d_ff = wdown_ref.shape[0]
    h = _modulated_norm(x, gnorm_ref[...], shift_ref[0], scale_ref[0]).astype(BF16)
    gu = jnp.dot(h, wgu_ref[...], preferred_element_type=F32)
    gate, up = gu[:, :d_ff], gu[:, d_ff:]
    act = (gate * _sigmoid(gate) * up).astype(BF16)
    x2 = x + gate_ref[0] * jnp.dot(act, wdown_ref[...], preferred_element_type=F32)
    ms = jnp.mean(x2 * x2, axis=-1, keepdims=True)
    o_ref[0] = x2 * lax.rsqrt(ms + EPS) * gfinal_ref[...]


def _ffn(x, shift, scale, gate, gnorm, wgu, wdown, gfinal):
    B, S, D = x.shape
    d_ff = wdown.shape[0]
    tm = TM_FFN
    vec = pl.BlockSpec((1, 1, D), lambda b, i: (b, 0, 0))
    return pl.pallas_call(
        _ffn_kernel,
        grid=(B, S // tm),
        in_specs=[pl.BlockSpec((1, tm, D), lambda b, i: (b, i, 0)), vec, vec, vec,
                  _const_spec((1, D)),
                  _const_spec((D, 2 * d_ff)), _const_spec((d_ff, D)), _const_spec((1, D))],
        out_specs=pl.BlockSpec((1, tm, D), lambda b, i: (b, i, 0)),
        out_shape=jax.ShapeDtypeStruct((B, S, D), F32),
        compiler_params=_params("parallel", "parallel"),
        name="ffn",
    )(x, shift, scale, gate, gnorm, wgu, wdown, gfinal)


def _rope_tables_t(seq, dtype):
    rows = seq // GRID_W
    t_row = jnp.repeat(jnp.arange(rows, dtype=F32), GRID_W)
    t_col = jnp.tile(jnp.arange(GRID_W, dtype=F32), rows)
    inv_freq = ROPE_THETA ** (-jnp.arange(0, ROPE_AXIS_DIM, 2, dtype=F32) / ROPE_AXIS_DIM)
    ang = jnp.concatenate([t_row[:, None] * inv_freq, t_col[:, None] * inv_freq], axis=-1)
    return jnp.cos(ang).astype(dtype).T, jnp.sin(ang).astype(dtype).T


def kernel(x, c, ctx, c_ctx, w_ada, b_ada, g_norm_mix, g_norm_ffn, w_in, b_gate, g_q, g_k, w_attn_proj,
           w_fourier_proj, w_o, w_gate_up, w_down, g_final):
    B, S, D = x.shape
    C = ctx.shape[1]
    assert D == D_MODEL and w_in.shape[0] == 1 and S % TM_PROJ == 0 and S % TQ == 0 and B < COND_ROWS
    layer = 0

    cond = jnp.zeros((COND_ROWS, D), F32).at[:B].set(c).at[B].set(c_ctx)
    mods = _adaln(cond, w_ada[layer], b_ada[layer])
    sh_m, sc_m, gt_m, sh_f, sc_f, gt_f = [m[:B, None, :] for m in jnp.split(mods, 6, axis=-1)]
    csh_m, csc_m = mods[B:B + 1, 0:D], mods[B:B + 1, D:2 * D]

    w_in_l = w_in[layer]
    wqkvT = w_in_l[:, :V_END].T.astype(BF16)
    wkvT = w_in_l[:, Q_W:V_END].T.astype(BF16)
    wf = w_in_l[:, V_END:F_END].astype(BF16)
    wg = w_in_l[:, F_END:].astype(BF16)
    gnorm_mix = g_norm_mix[layer].reshape(1, D)
    gnorm_ffn = g_norm_ffn[layer].reshape(1, D)

    gq_col = (g_q[layer] * (HEAD_DIM ** -0.5)).reshape(HEAD_DIM, 1)
    gk_col = g_k[layer].reshape(HEAD_DIM, 1)
    gq_b = jnp.broadcast_to(gq_col, (HEAD_DIM, TM_PROJ))
    gk_b = jnp.broadcast_to(gk_col, (HEAD_DIM, TM_PROJ))
    gk_ctx = jnp.broadcast_to(gk_col, (HEAD_DIM, C))
    cosT, sinT = _rope_tables_t(S, x.dtype)

    qT, k_lat, vT_lat, f = _in_proj(x, sh_m, sc_m, gnorm_mix, wqkvT, wf, gq_b, gk_b, cosT, sinT, 0)
    kc, vc = _ctx_kv(ctx, csh_m, csc_m, gnorm_mix, wkvT, gk_ctx)
    k_all = jnp.concatenate([k_lat, kc], axis=1)
    vT_all = jnp.concatenate([vT_lat, vc], axis=2)
    attn = _attention(qT, k_all, vT_all)
    four = _fourier(f)
    x1 = _merge(x, sh_m, sc_m, gt_m, gnorm_mix, wg, b_gate[layer].reshape(1, -1), attn, four,
                w_attn_proj[layer].astype(BF16), w_fourier_proj[layer].astype(BF16), w_o[layer].astype(BF16))
    return _ffn(x1, sh_f, sc_f, gt_f, gnorm_ffn, w_gate_up[layer].astype(BF16), w_down[layer].astype(BF16),
                g_final.reshape(1, D))
```

```python
import functools

import numpy as np
import jax
import jax.numpy as jnp
from jax import lax
from jax.experimental import pallas as pl
from jax.experimental.pallas import tpu as pltpu

D_MODEL = 1024
GRID_W = 64
HEAD_DIM = 64
N_HEADS = 16
N_KV_HEADS = 4
GROUP = N_HEADS // N_KV_HEADS
ROPE_AXIS_DIM = HEAD_DIM // 2
ROPE_THETA = 10000.0
N_FGROUPS = 4
FGROUP_DIM = 128
N_BRANCH = 2
Q_W = N_HEADS * HEAD_DIM
KV_W = N_KV_HEADS * HEAD_DIM
F_W = N_FGROUPS * FGROUP_DIM
K_END = Q_W + KV_W
V_END = K_END + KV_W
F_END = V_END + F_W
EPS = 1e-6

BF16 = jnp.bfloat16
F32 = jnp.float32

VMEM_LIMIT_BYTES = 56 * 1024 * 1024

COND_ROWS = 16
ONES_ROWS = 16
TM_PROJ = 512
TM_FFN = 256
TQ = 512
KEY_CHUNK = 256
TR_FOURIER = 512


def _params(*sem):
    return pltpu.CompilerParams(dimension_semantics=sem, vmem_limit_bytes=VMEM_LIMIT_BYTES)


def _const_spec(shape):
    nd = len(shape)
    return pl.BlockSpec(shape, lambda *_: (0,) * nd)


def _sigmoid(x):
    return 1.0 / (1.0 + jnp.exp(-x))


def _modulated_norm(x, gain, shift, scale):
    ms = jnp.mean(x * x, axis=-1, keepdims=True)
    y = x * lax.rsqrt(ms + EPS) * gain
    return y * (1.0 + scale) + shift


def _head_norm_t(zh, gain):
    ms = jnp.mean(zh * zh, axis=0, keepdims=True)
    return zh * lax.rsqrt(ms + EPS) * gain


def _rope_t(zn, cos, sin):
    half = HEAD_DIM // 2
    x1, x2 = zn[:half], zn[half:]
    return x1 * cos - x2 * sin, x2 * cos + x1 * sin


def _adaln_kernel(cond_ref, w_ref, b_ref, o_ref):
    c = cond_ref[...]
    s = c * _sigmoid(c)
    o_ref[...] = jnp.dot(s, w_ref[...], preferred_element_type=F32,
                         precision=lax.Precision.HIGHEST) + b_ref[...]


def _adaln(cond, w, b):
    n = w.shape[1]
    tn = n // 4
    return pl.pallas_call(
        _adaln_kernel,
        grid=(n // tn,),
        in_specs=[_const_spec((COND_ROWS, D_MODEL)),
                  pl.BlockSpec((D_MODEL, tn), lambda j: (0, j)),
                  pl.BlockSpec((1, tn), lambda j: (0, j))],
        out_specs=pl.BlockSpec((COND_ROWS, tn), lambda j: (0, j)),
        out_shape=jax.ShapeDtypeStruct((COND_ROWS, n), F32),
        compiler_params=_params("parallel"),
        name="adaln",
    )(cond, w, b.reshape(1, n))


def _in_proj_kernel(x_ref, shift_ref, scale_ref, gnorm_ref, wqkvT_ref, wf_ref, gq_ref, gk_ref,
                    cos_ref, sin_ref, qT_ref, k_ref, vT_ref, f_ref):
    h = _modulated_norm(x_ref[0], gnorm_ref[...], shift_ref[0], scale_ref[0]).astype(BF16)
    zT = lax.dot_general(wqkvT_ref[...], h, (((1,), (1,)), ((), ())), preferred_element_type=F32)
    cos, sin = cos_ref[...], sin_ref[...]
    gq, gk = gq_ref[...], gk_ref[...]
    half = HEAD_DIM // 2
    for i in range(N_HEADS):
        lo = i * HEAD_DIM
        a, b = _rope_t(_head_norm_t(zT[lo:lo + HEAD_DIM], gq), cos, sin)
        qT_ref[0, lo:lo + half, :] = a.astype(BF16)
        qT_ref[0, lo + half:lo + HEAD_DIM, :] = b.astype(BF16)
    k_parts = []
    for i in range(N_KV_HEADS):
        lo = Q_W + i * HEAD_DIM
        k_parts.extend(_rope_t(_head_norm_t(zT[lo:lo + HEAD_DIM], gk), cos, sin))
    k_ref[0] = jnp.concatenate(k_parts, axis=0).T.astype(BF16)
    vT_ref[0] = zT[K_END:V_END].astype(BF16)
    f_ref[0] = jnp.dot(h, wf_ref[...], preferred_element_type=F32).astype(BF16)


def _in_proj(x, shift, scale, gnorm, wqkvT, wf, gq_b, gk_b, cosT, sinT, n_ctx):
    B, S, D = x.shape
    tm = TM_PROJ
    half = HEAD_DIM // 2
    return pl.pallas_call(
        _in_proj_kernel,
        grid=(B, S // tm),
        in_specs=[pl.BlockSpec((1, tm, D), lambda b, i: (b, i, 0)),
                  pl.BlockSpec((1, 1, D), lambda b, i: (b, 0, 0)),
                  pl.BlockSpec((1, 1, D), lambda b, i: (b, 0, 0)),
                  _const_spec((1, D)),
                  _const_spec((V_END, D)), _const_spec((D, F_W)),
                  _const_spec((HEAD_DIM, tm)), _const_spec((HEAD_DIM, tm)),
                  pl.BlockSpec((half, tm), lambda b, i: (0, i)),
                  pl.BlockSpec((half, tm), lambda b, i: (0, i))],
        out_specs=[pl.BlockSpec((1, Q_W, tm), lambda b, i: (b, 0, i)),
                   pl.BlockSpec((1, tm, KV_W), lambda b, i: (b, i, 0)),
                   pl.BlockSpec((1, KV_W, tm), lambda b, i: (b, 0, i)),
                   pl.BlockSpec((1, tm, F_W), lambda b, i: (b, i, 0))],
        out_shape=[jax.ShapeDtypeStruct((B, Q_W, S), BF16),
                   jax.ShapeDtypeStruct((B, S + n_ctx, KV_W), BF16),
                   jax.ShapeDtypeStruct((B, KV_W, S + n_ctx), BF16),
                   jax.ShapeDtypeStruct((B, S, F_W), BF16)],
        compiler_params=_params("parallel", "parallel"),
        name="in_proj",
    )(x, shift, scale, gnorm, wqkvT, wf, gq_b, gk_b, cosT, sinT)


def _ctx_kv_kernel(x_ref, shift_ref, scale_ref, gnorm_ref, wkvT_ref, gk_ref, k_ref, vT_ref):
    h = _modulated_norm(x_ref[0], gnorm_ref[...], shift_ref[...], scale_ref[...]).astype(BF16)
    zT = lax.dot_general(wkvT_ref[...], h, (((1,), (1,)), ((), ())), preferred_element_type=F32)
    gk = gk_ref[...]
    kT = jnp.concatenate(
        [_head_norm_t(zT[i * HEAD_DIM:(i + 1) * HEAD_DIM], gk) for i in range(N_KV_HEADS)], axis=0)
    k_ref[0] = kT.T.astype(BF16)
    vT_ref[0] = zT[KV_W:].astype(BF16)


def _ctx_kv(ctx, shift, scale, gnorm, wkvT, gk_b):
    B, C, D = ctx.shape
    return pl.pallas_call(
        _ctx_kv_kernel,
        grid=(B,),
        in_specs=[pl.BlockSpec((1, C, D), lambda b: (b, 0, 0)),
                  _const_spec((1, D)), _const_spec((1, D)), _const_spec((1, D)),
                  _const_spec((2 * KV_W, D)), _const_spec((HEAD_DIM, C))],
        out_specs=[pl.BlockSpec((1, C, KV_W), lambda b: (b, 0, 0)),
                   pl.BlockSpec((1, KV_W, C), lambda b: (b, 0, 0))],
        out_shape=[jax.ShapeDtypeStruct((B, C, KV_W), BF16),
                   jax.ShapeDtypeStruct((B, KV_W, C), BF16)],
        compiler_params=_params("parallel"),
        name="ctx_kv",
    )(ctx, shift, scale, gnorm, wkvT, gk_b)


def _attention_kernel(qT_ref, k_ref, vT_ref, o_ref, vaug_ref, qz_ref, s0_ref, s1_ref, m0_ref, m1_ref, oT_ref):
    qi = pl.program_id(1)
    tq = qT_ref.shape[2]
    n_chunks = k_ref.shape[1] // KEY_CHUNK

    @pl.when(qi == 0)
    def _():
        ones = jnp.ones((ONES_ROWS, KEY_CHUNK), BF16)
        for g in range(N_KV_HEADS):
            for c in range(n_chunks):
                vaug_ref[g, c, :HEAD_DIM, :] = vT_ref[0, g * HEAD_DIM:(g + 1) * HEAD_DIM,
                                                      c * KEY_CHUNK:(c + 1) * KEY_CHUNK]
                vaug_ref[g, c, HEAD_DIM:, :] = ones

    def head_rows(t):
        return slice(t * HEAD_DIM, (t + 1) * HEAD_DIM)

    def load_qz(t):
        qz_ref[...] = jnp.zeros_like(qz_ref)
        qz_ref[head_rows(t // GROUP), :] = qT_ref[0, head_rows(t), :]

    def step(t_b, sb, sa):
        if sa is not None:
            load_qz(t_b + 1)
        m8, oa = jnp.full((8, tq), -jnp.inf, F32), None
        for c in range(n_chunks):
            rows = slice(c * KEY_CHUNK, (c + 1) * KEY_CHUNK)
            srows = pl.ds(pl.multiple_of(row0 + c * KEY_CHUNK, KEY_CHUNK), KEY_CHUNK)
            if sb is not None:
                p = jnp.exp(sb[0][srows, :] - sb[1][0:1, :]).astype(BF16)
                d = jnp.dot(vaug_ref[t_b // GROUP, c], p, preferred_element_type=F32)
                oa = d if oa is None else oa + d
            if sa is not None:
                s = jnp.dot(k_ref[0, rows, :], qz_ref[...], preferred_element_type=F32)
                sa[0][srows, :] = s
                m8 = jnp.maximum(m8, jnp.max(s.reshape(KEY_CHUNK // 8, 8, tq), axis=0))
        if sa is not None:
            sa[1][...] = jnp.broadcast_to(jnp.max(m8, axis=0, keepdims=True), (8, tq))
        if sb is not None:
            oT_ref[head_rows(t_b), :] = oa[:HEAD_DIM] / oa[HEAD_DIM:HEAD_DIM + 1]

    buf0, buf1 = (s0_ref, m0_ref), (s1_ref, m1_ref)
    row0 = jnp.minimum(pl.program_id(0), 0)
    step(-1, None, buf0)

    bufs = (buf0, buf1)
    for t in range(N_HEADS - 1):
        step(t, bufs[t % 2], bufs[(t + 1) % 2])
    step(N_HEADS - 1, bufs[(N_HEADS - 1) % 2], None)
    o_ref[0] = oT_ref[...].T.astype(BF16)


def _attention(qT, k_all, vT_all):
    B, _, S = qT.shape
    T = k_all.shape[1]
    assert T % KEY_CHUNK == 0
    n_chunks = T // KEY_CHUNK
    return pl.pallas_call(
        _attention_kernel,
        grid=(B, S // TQ),
        in_specs=[pl.BlockSpec((1, Q_W, TQ), lambda b, i: (b, 0, i)),
                  pl.BlockSpec((1, T, KV_W), lambda b, i: (b, 0, 0)),
                  pl.BlockSpec((1, KV_W, T), lambda b, i: (b, 0, 0))],
        out_specs=pl.BlockSpec((1, TQ, Q_W), lambda b, i: (b, i, 0)),
        out_shape=jax.ShapeDtypeStruct((B, S, Q_W), BF16),
        scratch_shapes=[pltpu.VMEM((N_KV_HEADS, n_chunks, HEAD_DIM + ONES_ROWS, KEY_CHUNK), BF16),
                        pltpu.VMEM((KV_W, TQ), BF16),
                        pltpu.VMEM((T, TQ), F32), pltpu.VMEM((T, TQ), F32),
                        pltpu.VMEM((8, TQ), F32), pltpu.VMEM((8, TQ), F32),
                        pltpu.VMEM((Q_W, TQ), F32)],
        compiler_params=_params("parallel", "arbitrary"),
        name="attention",
    )(qT, k_all, vT_all)


@functools.lru_cache(maxsize=None)
def _dft_tables(n_pos):
    def cos_sin(n):
        idx = np.arange(n, dtype=np.int64)
        ang = 2.0 * np.pi * ((idx[:, None] * idx[None, :]) % n).astype(np.float64) / n
        return np.cos(ang), np.sin(ang)
    cc, sc = cos_sin(FGROUP_DIM)
    cl, sl = cos_sin(n_pos)
    chan = np.concatenate([cc, sc], axis=1).astype(np.float32)
    pos = np.concatenate([cl, -sl], axis=1).astype(np.float32)
    return chan, pos


def _fourier_kernel(f_ref, chan_ref, pos_ref, o_ref, z_ref, *, scale):
    n_pos = f_ref.shape[1]

    @pl.when(pl.program_id(1) == 0)
    def _():
        for g in range(N_FGROUPS):
            cols = slice(g * FGROUP_DIM, (g + 1) * FGROUP_DIM)
            xcs = jnp.dot(f_ref[0, :, cols], chan_ref[...], preferred_element_type=F32)
            z_ref[0:n_pos, cols] = xcs[:, :FGROUP_DIM].astype(BF16)
            z_ref[n_pos:2 * n_pos, cols] = xcs[:, FGROUP_DIM:].astype(BF16)

    y = jnp.dot(pos_ref[...], z_ref[...], preferred_element_type=F32)
    o_ref[0] = (y * scale).astype(BF16)


def _fourier(f):
    B, S, W = f.shape
    chan, pos = _dft_tables(S)
    tr = TR_FOURIER
    scale = float(1.0 / np.sqrt(S * FGROUP_DIM))
    return pl.pallas_call(
        functools.partial(_fourier_kernel, scale=scale),
        grid=(B, S // tr),
        in_specs=[pl.BlockSpec((1, S, W), lambda b, i: (b, 0, 0)),
                  _const_spec((FGROUP_DIM, 2 * FGROUP_DIM)),
                  pl.BlockSpec((tr, 2 * S), lambda b, i: (i, 0))],
        out_specs=pl.BlockSpec((1, tr, W), lambda b, i: (b, i, 0)),
        out_shape=jax.ShapeDtypeStruct((B, S, W), BF16),
        scratch_shapes=[pltpu.VMEM((2 * S, W), BF16)],
        compiler_params=_params("parallel", "arbitrary"),
        name="fourier",
    )(f, jnp.asarray(chan).astype(BF16), jnp.asarray(pos).astype(BF16))


def _merge_kernel(x_ref, shift_ref, scale_ref, gate_ref, gnorm_ref, wg_ref, bg_ref, attn_ref, four_ref,
                  wap_ref, wfp_ref, wo_ref, o_ref):
    x = x_ref[0]
    h = _modulated_norm(x, gnorm_ref[...], shift_ref[0], scale_ref[0]).astype(BF16)
    gates = _sigmoid(jnp.dot(h, wg_ref[...], preferred_element_type=F32) + bg_ref[...])
    a = jnp.dot(attn_ref[0], wap_ref[...], preferred_element_type=F32)
    fo = jnp.dot(four_ref[0], wfp_ref[...], preferred_element_type=F32)
    merged = gates[:, :D_MODEL] * a + gates[:, D_MODEL:] * fo
    mix = jnp.dot(merged.astype(BF16), wo_ref[...], preferred_element_type=F32)
    o_ref[0] = x + gate_ref[0] * mix


def _merge(x, shift, scale, gate, gnorm, wg, bg, attn, four, wap, wfp, wo):
    B, S, D = x.shape
    tm = TM_PROJ
    vec = pl.BlockSpec((1, 1, D), lambda b, i: (b, 0, 0))
    return pl.pallas_call(
        _merge_kernel,
        grid=(B, S // tm),
        in_specs=[pl.BlockSpec((1, tm, D), lambda b, i: (b, i, 0)), vec, vec, vec,
                  _const_spec((1, D)),
                  _const_spec((D, N_BRANCH * D)), _const_spec((1, N_BRANCH * D)),
                  pl.BlockSpec((1, tm, Q_W), lambda b, i: (b, i, 0)),
                  pl.BlockSpec((1, tm, F_W), lambda b, i: (b, i, 0)),
                  _const_spec((Q_W, D)), _const_spec((F_W, D)), _const_spec((D, D))],
        out_specs=pl.BlockSpec((1, tm, D), lambda b, i: (b, i, 0)),
        out_shape=jax.ShapeDtypeStruct((B, S, D), F32),
        compiler_params=_params("parallel", "parallel"),
        name="merge",
    )(x, shift, scale, gate, gnorm, wg, bg, attn, four, wap, wfp, wo)


def _ffn_kernel(x_ref, shift_ref, scale_ref, gate_ref, gnorm_ref, wgu_ref, wdown_ref, gfinal_ref, o_ref):
    x = x_ref[0]
    d_ff = wdown_ref.shape[0]
    h = _modulated_norm(x, gnorm_ref[...], shift_ref[0], scale_ref[0]).astype(BF16)
    gu = jnp.dot(h, wgu_ref[...], preferred_element_type=F32)
    gate, up = gu[:, :d_ff], gu[:, d_ff:]
    act = (gate * _sigmoid(gate) * up).astype(BF16)
    x2 = x + gate_ref[0] * jnp.dot(act, wdown_ref[...], preferred_element_type=F32)
    ms = jnp.mean(x2 * x2, axis=-1, keepdims=True)
    o_ref[0] = x2 * lax.rsqrt(ms + EPS) * gfinal_ref[...]


def _ffn(x, shift, scale, gate, gnorm, wgu, wdown, gfinal):
    B, S, D = x.shape
    d_ff = wdown.shape[0]
    tm = TM_FFN
    vec = pl.BlockSpec((1, 1, D), lambda b, i: (b, 0, 0))
    return pl.pallas_call(
        _ffn_kernel,
        grid=(B, S // tm),
        in_specs=[pl.BlockSpec((1, tm, D), lambda b, i: (b, i, 0)), vec, vec, vec,
                  _const_spec((1, D)),
                  _const_spec((D, 2 * d_ff)), _const_spec((d_ff, D)), _const_spec((1, D))],
        out_specs=pl.BlockSpec((1, tm, D), lambda b, i: (b, i, 0)),
        out_shape=jax.ShapeDtypeStruct((B, S, D), F32),
        compiler_params=_params("parallel", "parallel"),
        name="ffn",
    )(x, shift, scale, gate, gnorm, wgu, wdown, gfinal)


def _rope_tables_t(seq, dtype):
    rows = seq // GRID_W
    t_row = jnp.repeat(jnp.arange(rows, dtype=F32), GRID_W)
    t_col = jnp.tile(jnp.arange(GRID_W, dtype=F32), rows)
    inv_freq = ROPE_THETA ** (-jnp.arange(0, ROPE_AXIS_DIM, 2, dtype=F32) / ROPE_AXIS_DIM)
    ang = jnp.concatenate([t_row[:, None] * inv_freq, t_col[:, None] * inv_freq], axis=-1)
    return jnp.cos(ang).astype(dtype).T, jnp.sin(ang).astype(dtype).T


def kernel(x, c, ctx, c_ctx, w_ada, b_ada, g_norm_mix, g_norm_ffn, w_in, b_gate, g_q, g_k, w_attn_proj,
           w_fourier_proj, w_o, w_gate_up, w_down, g_final):
    B, S, D = x.shape
    C = ctx.shape[1]
    assert D == D_MODEL and w_in.shape[0] == 1 and S % TM_PROJ == 0 and S % TQ == 0 and B < COND_ROWS
    layer = 0

    cond = jnp.zeros((COND_ROWS, D), F32).at[:B].set(c).at[B].set(c_ctx)
    mods = _adaln(cond, w_ada[layer], b_ada[layer])
    sh_m, sc_m, gt_m, sh_f, sc_f, gt_f = [m[:B, None, :] for m in jnp.split(mods, 6, axis=-1)]
    csh_m, csc_m = mods[B:B + 1, 0:D], mods[B:B + 1, D:2 * D]

    w_in_l = w_in[layer]
    wqkvT = w_in_l[:, :V_END].T.astype(BF16)
    wkvT = w_in_l[:, Q_W:V_END].T.astype(BF16)
    wf = w_in_l[:, V_END:F_END].astype(BF16)
    wg = w_in_l[:, F_END:].astype(BF16)
    gnorm_mix = g_norm_mix[layer].reshape(1, D)
    gnorm_ffn = g_norm_ffn[layer].reshape(1, D)

    gq_col = (g_q[layer] * (HEAD_DIM ** -0.5)).reshape(HEAD_DIM, 1)
    gk_col = g_k[layer].reshape(HEAD_DIM, 1)
    gq_b = jnp.broadcast_to(gq_col, (HEAD_DIM, TM_PROJ))
    gk_b = jnp.broadcast_to(gk_col, (HEAD_DIM, TM_PROJ))
    gk_ctx = jnp.broadcast_to(gk_col, (HEAD_DIM, C))
    cosT, sinT = _rope_tables_t(S, x.dtype)

    qT, k_lat, vT_lat, f = _in_proj(x, sh_m, sc_m, gnorm_mix, wqkvT, wf, gq_b, gk_b, cosT, sinT, 0)
    kc, vc = _ctx_kv(ctx, csh_m, csc_m, gnorm_mix, wkvT, gk_ctx)
    k_all = jnp.concatenate([k_lat, kc], axis=1)
    vT_all = jnp.concatenate([vT_lat, vc], axis=2)
    attn = _attention(qT, k_all, vT_all)
    four = _fourier(f)
    x1 = _merge(x, sh_m, sc_m, gt_m, gnorm_mix, wg, b_gate[layer].reshape(1, -1), attn, four,
                w_attn_proj[layer].astype(BF16), w_fourier_proj[layer].astype(BF16), w_o[layer].astype(BF16))
    return _ffn(x1, sh_f, sc_f, gt_f, gnorm_ffn, w_gate_up[layer].astype(BF16), w_down[layer].astype(BF16),
                g_final.reshape(1, D))
```

```python
import functools

import numpy as np
import jax
import jax.numpy as jnp
from jax import lax
from jax.experimental import pallas as pl
from jax.experimental.pallas import tpu as pltpu

D_MODEL = 1024
GRID_W = 64
HEAD_DIM = 64
N_HEADS = 16
N_KV_HEADS = 4
GROUP = N_HEADS // N_KV_HEADS
ROPE_AXIS_DIM = HEAD_DIM // 2
ROPE_THETA = 10000.0
N_FGROUPS = 4
FGROUP_DIM = 128
N_BRANCH = 2
Q_W = N_HEADS * HEAD_DIM
KV_W = N_KV_HEADS * HEAD_DIM
F_W = N_FGROUPS * FGROUP_DIM
K_END = Q_W + KV_W
V_END = K_END + KV_W
F_END = V_END + F_W
EPS = 1e-6
LOG2_E = 1.4426950408889634

BF16 = jnp.bfloat16
F32 = jnp.float32

VMEM_LIMIT_BYTES = 56 * 1024 * 1024

COND_ROWS = 16
ONES_ROWS = 16
TM_PROJ = 512
TM_FFN = 256
TQ = 512
KEY_CHUNK = 256
TR_FOURIER = 512


def _params(*sem):
    return pltpu.CompilerParams(dimension_semantics=sem, vmem_limit_bytes=VMEM_LIMIT_BYTES)


def _const_spec(shape):
    nd = len(shape)
    return pl.BlockSpec(shape, lambda *_: (0,) * nd)


def _sigmoid(x):
    return 1.0 / (1.0 + jnp.exp(-x))


def _modulated_norm(x, gain, shift, scale):
    ms = jnp.mean(x * x, axis=-1, keepdims=True)
    y = x * lax.rsqrt(ms + EPS) * gain
    return y * (1.0 + scale) + shift


def _head_norm_t(zh, gain):
    ms = jnp.mean(zh * zh, axis=0, keepdims=True)
    return zh * lax.rsqrt(ms + EPS) * gain


def _rope_t(zn, cos, sin):
    half = HEAD_DIM // 2
    x1, x2 = zn[:half], zn[half:]
    return x1 * cos - x2 * sin, x2 * cos + x1 * sin


def _adaln_kernel(cond_ref, w_ref, b_ref, o_ref):
    c = cond_ref[...]
    s = c * _sigmoid(c)
    o_ref[...] = jnp.dot(s, w_ref[...], preferred_element_type=F32,
                         precision=lax.Precision.HIGHEST) + b_ref[...]


def _adaln(cond, w, b):
    n = w.shape[1]
    tn = n // 4
    return pl.pallas_call(
        _adaln_kernel,
        grid=(n // tn,),
        in_specs=[_const_spec((COND_ROWS, D_MODEL)),
                  pl.BlockSpec((D_MODEL, tn), lambda j: (0, j)),
                  pl.BlockSpec((1, tn), lambda j: (0, j))],
        out_specs=pl.BlockSpec((COND_ROWS, tn), lambda j: (0, j)),
        out_shape=jax.ShapeDtypeStruct((COND_ROWS, n), F32),
        compiler_params=_params("parallel"),
        name="adaln",
    )(cond, w, b.reshape(1, n))


def _in_proj_kernel(x_ref, shift_ref, scale_ref, gnorm_ref, wqkvT_ref, wf_ref, gq_ref, gk_ref,
                    cos_ref, sin_ref, qT_ref, k_ref, vT_ref, f_ref):
    h = _modulated_norm(x_ref[0], gnorm_ref[...], shift_ref[0], scale_ref[0]).astype(BF16)
    zT = lax.dot_general(wqkvT_ref[...], h, (((1,), (1,)), ((), ())), preferred_element_type=F32)
    cos, sin = cos_ref[...], sin_ref[...]
    gq, gk = gq_ref[...], gk_ref[...]
    half = HEAD_DIM // 2
    for i in range(N_HEADS):
        lo = i * HEAD_DIM
        a, b = _rope_t(_head_norm_t(zT[lo:lo + HEAD_DIM], gq), cos, sin)
        qT_ref[0, lo:lo + half, :] = a.astype(BF16)
        qT_ref[0, lo + half:lo + HEAD_DIM, :] = b.astype(BF16)
    k_parts = []
    for i in range(N_KV_HEADS):
        lo = Q_W + i * HEAD_DIM
        k_parts.extend(_rope_t(_head_norm_t(zT[lo:lo + HEAD_DIM], gk), cos, sin))
    k_ref[0] = jnp.concatenate(k_parts, axis=0).T.astype(BF16)
    vT_ref[0] = zT[K_END:V_END].astype(BF16)
    f_ref[0] = jnp.dot(h, wf_ref[...], preferred_element_type=F32).astype(BF16)


def _in_proj(x, shift, scale, gnorm, wqkvT, wf, gq_b, gk_b, cosT, sinT, n_ctx):
    B, S, D = x.shape
    tm = TM_PROJ
    half = HEAD_DIM // 2
    return pl.pallas_call(
        _in_proj_kernel,
        grid=(B, S // tm),
        in_specs=[pl.BlockSpec((1, tm, D), lambda b, i: (b, i, 0)),
                  pl.BlockSpec((1, 1, D), lambda b, i: (b, 0, 0)),
                  pl.BlockSpec((1, 1, D), lambda b, i: (b, 0, 0)),
                  _const_spec((1, D)),
                  _const_spec((V_END, D)), _const_spec((D, F_W)),
                  _const_spec((HEAD_DIM, tm)), _const_spec((HEAD_DIM, tm)),
                  pl.BlockSpec((half, tm), lambda b, i: (0, i)),
                  pl.BlockSpec((half, tm), lambda b, i: (0, i))],
        out_specs=[pl.BlockSpec((1, Q_W, tm), lambda b, i: (b, 0, i)),
                   pl.BlockSpec((1, tm, KV_W), lambda b, i: (b, i, 0)),
                   pl.BlockSpec((1, KV_W, tm), lambda b, i: (b, 0, i)),
                   pl.BlockSpec((1, tm, F_W), lambda b, i: (b, i, 0))],
        out_shape=[jax.ShapeDtypeStruct((B, Q_W, S), BF16),
                   jax.ShapeDtypeStruct((B, S + n_ctx, KV_W), BF16),
                   jax.ShapeDtypeStruct((B, KV_W, S + n_ctx), BF16),
                   jax.ShapeDtypeStruct((B, S, F_W), BF16)],
        compiler_params=_params("parallel", "parallel"),
        name="in_proj",
    )(x, shift, scale, gnorm, wqkvT, wf, gq_b, gk_b, cosT, sinT)


def _ctx_kv_kernel(x_ref, shift_ref, scale_ref, gnorm_ref, wkvT_ref, gk_ref, k_ref, vT_ref):
    h = _modulated_norm(x_ref[0], gnorm_ref[...], shift_ref[...], scale_ref[...]).astype(BF16)
    zT = lax.dot_general(wkvT_ref[...], h, (((1,), (1,)), ((), ())), preferred_element_type=F32)
    gk = gk_ref[...]
    kT = jnp.concatenate(
        [_head_norm_t(zT[i * HEAD_DIM:(i + 1) * HEAD_DIM], gk) for i in range(N_KV_HEADS)], axis=0)
    k_ref[0] = kT.T.astype(BF16)
    vT_ref[0] = zT[KV_W:].astype(BF16)


def _ctx_kv(ctx, shift, scale, gnorm, wkvT, gk_b):
    B, C, D = ctx.shape
    return pl.pallas_call(
        _ctx_kv_kernel,
        grid=(B,),
        in_specs=[pl.BlockSpec((1, C, D), lambda b: (b, 0, 0)),
                  _const_spec((1, D)), _const_spec((1, D)), _const_spec((1, D)),
                  _const_spec((2 * KV_W, D)), _const_spec((HEAD_DIM, C))],
        out_specs=[pl.BlockSpec((1, C, KV_W), lambda b: (b, 0, 0)),
                   pl.BlockSpec((1, KV_W, C), lambda b: (b, 0, 0))],
        out_shape=[jax.ShapeDtypeStruct((B, C, KV_W), BF16),
                   jax.ShapeDtypeStruct((B, KV_W, C), BF16)],
        compiler_params=_params("parallel"),
        name="ctx_kv",
    )(ctx, shift, scale, gnorm, wkvT, gk_b)


def _attention_kernel(qT_ref, k_ref, vT_ref, o_ref, vaug_ref, qz_ref, s0_ref, s1_ref, m0_ref, m1_ref, oT_ref):
    qi = pl.program_id(1)
    tq = qT_ref.shape[2]
    n_chunks = k_ref.shape[1] // KEY_CHUNK

    @pl.when(qi == 0)
    def _():
        ones = jnp.ones((ONES_ROWS, KEY_CHUNK), BF16)
        for g in range(N_KV_HEADS):
            for c in range(n_chunks):
                vaug_ref[g, c, :HEAD_DIM, :] = vT_ref[0, g * HEAD_DIM:(g + 1) * HEAD_DIM,
                                                      c * KEY_CHUNK:(c + 1) * KEY_CHUNK]
                vaug_ref[g, c, HEAD_DIM:, :] = ones

    def head_rows(t):
        return slice(t * HEAD_DIM, (t + 1) * HEAD_DIM)

    def load_qz(t):
        qz_ref[...] = jnp.zeros_like(qz_ref)
        qz_ref[head_rows(t // GROUP), :] = qT_ref[0, head_rows(t), :]

    def step(t_b, sb, sa):
        if sa is not None:
            load_qz(t_b + 1)
        m8, oa = jnp.full((8, tq), -jnp.inf, F32), None
        for c in range(n_chunks):
            rows = slice(c * KEY_CHUNK, (c + 1) * KEY_CHUNK)
            srows = pl.ds(pl.multiple_of(row0 + c * KEY_CHUNK, KEY_CHUNK), KEY_CHUNK)
            if sb is not None:
                p = jnp.exp2(sb[0][srows, :] - sb[1][0:1, :]).astype(BF16)
                d = jnp.dot(vaug_ref[t_b // GROUP, c], p, preferred_element_type=F32)
                oa = d if oa is None else oa + d
            if sa is not None:
                s = jnp.dot(k_ref[0, rows, :], qz_ref[...], preferred_element_type=F32)
                sa[0][srows, :] = s
                m8 = jnp.maximum(m8, jnp.max(s.reshape(KEY_CHUNK // 8, 8, tq), axis=0))
        if sa is not None:
            sa[1][...] = jnp.broadcast_to(jnp.max(m8, axis=0, keepdims=True), (8, tq))
        if sb is not None:
            oT_ref[head_rows(t_b), :] = oa[:HEAD_DIM] / oa[HEAD_DIM:HEAD_DIM + 1]

    buf0, buf1 = (s0_ref, m0_ref), (s1_ref, m1_ref)
    row0 = jnp.minimum(pl.program_id(0), 0)
    step(-1, None, buf0)

    bufs = (buf0, buf1)
    for t in range(N_HEADS - 1):
        step(t, bufs[t % 2], bufs[(t + 1) % 2])
    step(N_HEADS - 1, bufs[(N_HEADS - 1) % 2], None)
    o_ref[0] = oT_ref[...].T.astype(BF16)


def _attention(qT, k_all, vT_all):
    B, _, S = qT.shape
    T = k_all.shape[1]
    assert T % KEY_CHUNK == 0
    n_chunks = T // KEY_CHUNK
    return pl.pallas_call(
        _attention_kernel,
        grid=(B, S // TQ),
        in_specs=[pl.BlockSpec((1, Q_W, TQ), lambda b, i: (b, 0, i)),
                  pl.BlockSpec((1, T, KV_W), lambda b, i: (b, 0, 0)),
                  pl.BlockSpec((1, KV_W, T), lambda b, i: (b, 0, 0))],
        out_specs=pl.BlockSpec((1, TQ, Q_W), lambda b, i: (b, i, 0)),
        out_shape=jax.ShapeDtypeStruct((B, S, Q_W), BF16),
        scratch_shapes=[pltpu.VMEM((N_KV_HEADS, n_chunks, HEAD_DIM + ONES_ROWS, KEY_CHUNK), BF16),
                        pltpu.VMEM((KV_W, TQ), BF16),
                        pltpu.VMEM((T, TQ), F32), pltpu.VMEM((T, TQ), F32),
                        pltpu.VMEM((8, TQ), F32), pltpu.VMEM((8, TQ), F32),
                        pltpu.VMEM((Q_W, TQ), F32)],
        compiler_params=_params("parallel", "arbitrary"),
        name="attention",
    )(qT, k_all, vT_all)


@functools.lru_cache(maxsize=None)
def _dft_tables(n_pos):
    def cos_sin(n):
        idx = np.arange(n, dtype=np.int64)
        ang = 2.0 * np.pi * ((idx[:, None] * idx[None, :]) % n).astype(np.float64) / n
        return np.cos(ang), np.sin(ang)
    cc, sc = cos_sin(FGROUP_DIM)
    cl, sl = cos_sin(n_pos)
    chan = np.concatenate([cc, sc], axis=1).astype(np.float32)
    pos = np.concatenate([cl, -sl], axis=1).astype(np.float32)
    return chan, pos


def _fourier_kernel(f_ref, chan_ref, pos_ref, o_ref, z_ref, *, scale):
    n_pos = f_ref.shape[1]

    @pl.when(pl.program_id(1) == 0)
    def _():
        for g in range(N_FGROUPS):
            cols = slice(g * FGROUP_DIM, (g + 1) * FGROUP_DIM)
            xcs = jnp.dot(f_ref[0, :, cols], chan_ref[...], preferred_element_type=F32)
            z_ref[0:n_pos, cols] = xcs[:, :FGROUP_DIM].astype(BF16)
            z_ref[n_pos:2 * n_pos, cols] = xcs[:, FGROUP_DIM:].astype(BF16)

    y = jnp.dot(pos_ref[...], z_ref[...], preferred_element_type=F32)
    o_ref[0] = (y * scale).astype(BF16)


def _fourier(f):
    B, S, W = f.shape
    chan, pos = _dft_tables(S)
    tr = TR_FOURIER
    scale = float(1.0 / np.sqrt(S * FGROUP_DIM))
    return pl.pallas_call(
        functools.partial(_fourier_kernel, scale=scale),
        grid=(B, S // tr),
        in_specs=[pl.BlockSpec((1, S, W), lambda b, i: (b, 0, 0)),
                  _const_spec((FGROUP_DIM, 2 * FGROUP_DIM)),
                  pl.BlockSpec((tr, 2 * S), lambda b, i: (i, 0))],
        out_specs=pl.BlockSpec((1, tr, W), lambda b, i: (b, i, 0)),
        out_shape=jax.ShapeDtypeStruct((B, S, W), BF16),
        scratch_shapes=[pltpu.VMEM((2 * S, W), BF16)],
        compiler_params=_params("parallel", "arbitrary"),
        name="fourier",
    )(f, jnp.asarray(chan).astype(BF16), jnp.asarray(pos).astype(BF16))


def _merge_kernel(x_ref, shift_ref, scale_ref, gate_ref, gnorm_ref, wg_ref, bg_ref, attn_ref, four_ref,
                  wap_ref, wfp_ref, wo_ref, o_ref):
    x = x_ref[0]
    h = _modulated_norm(x, gnorm_ref[...], shift_ref[0], scale_ref[0]).astype(BF16)
    gates = _sigmoid(jnp.dot(h, wg_ref[...], preferred_element_type=F32) + bg_ref[...])
    a = jnp.dot(attn_ref[0], wap_ref[...], preferred_element_type=F32)
    fo = jnp.dot(four_ref[0], wfp_ref[...], preferred_element_type=F32)
    merged = gates[:, :D_MODEL] * a + gates[:, D_MODEL:] * fo
    mix = jnp.dot(merged.astype(BF16), wo_ref[...], preferred_element_type=F32)
    o_ref[0] = x + gate_ref[0] * mix


def _merge(x, shift, scale, gate, gnorm, wg, bg, attn, four, wap, wfp, wo):
    B, S, D = x.shape
    tm = TM_PROJ
    vec = pl.BlockSpec((1, 1, D), lambda b, i: (b, 0, 0))
    return pl.pallas_call(
        _merge_kernel,
        grid=(B, S // tm),
        in_specs=[pl.BlockSpec((1, tm, D), lambda b, i: (b, i, 0)), vec, vec, vec,
                  _const_spec((1, D)),
                  _const_spec((D, N_BRANCH * D)), _const_spec((1, N_BRANCH * D)),
                  pl.BlockSpec((1, tm, Q_W), lambda b, i: (b, i, 0)),
                  pl.BlockSpec((1, tm, F_W), lambda b, i: (b, i, 0)),
                  _const_spec((Q_W, D)), _const_spec((F_W, D)), _const_spec((D, D))],
        out_specs=pl.BlockSpec((1, tm, D), lambda b, i: (b, i, 0)),
        out_shape=jax.ShapeDtypeStruct((B, S, D), F32),
        compiler_params=_params("parallel", "parallel"),
        name="merge",
    )(x, shift, scale, gate, gnorm, wg, bg, attn, four, wap, wfp, wo)


def _ffn_kernel(x_ref, shift_ref, scale_ref, gate_ref, gnorm_ref, wgu_ref, wdown_ref, gfinal_ref, o_ref):
    x = x_ref[0]
    d_ff = wdown_ref.shape[0]
    h = _modulated_norm(x, gnorm_ref[...], shift_ref[0], scale_ref[0]).astype(BF16)
    gu = jnp.dot(h, wgu_ref[...], preferred_element_type=F32)
    gate, up = gu[:, :d_ff], gu[:, d_ff:]
    act = (gate * _sigmoid(gate) * up).astype(BF16)
    x2 = x + gate_ref[0] * jnp.dot(act, wdown_ref[...], preferred_element_type=F32)
    ms = jnp.mean(x2 * x2, axis=-1, keepdims=True)
    o_ref[0] = x2 * lax.rsqrt(ms + EPS) * gfinal_ref[...]


def _ffn(x, shift, scale, gate, gnorm, wgu, wdown, gfinal):
    B, S, D = x.shape
    d_ff = wdown.shape[0]
    tm = TM_FFN
    vec = pl.BlockSpec((1, 1, D), lambda b, i: (b, 0, 0))
    return pl.pallas_call(
        _ffn_kernel,
        grid=(B, S // tm),
        in_specs=[pl.BlockSpec((1, tm, D), lambda b, i: (b, i, 0)), vec, vec, vec,
                  _const_spec((1, D)),
                  _const_spec((D, 2 * d_ff)), _const_spec((d_ff, D)), _const_spec((1, D))],
        out_specs=pl.BlockSpec((1, tm, D), lambda b, i: (b, i, 0)),
        out_shape=jax.ShapeDtypeStruct((B, S, D), F32),
        compiler_params=_params("parallel", "parallel"),
        name="ffn",
    )(x, shift, scale, gate, gnorm, wgu, wdown, gfinal)


def _rope_tables_t(seq, dtype):
    rows = seq // GRID_W
    t_row = jnp.repeat(jnp.arange(rows, dtype=F32), GRID_W)
    t_col = jnp.tile(jnp.arange(GRID_W, dtype=F32), rows)
    inv_freq = ROPE_THETA ** (-jnp.arange(0, ROPE_AXIS_DIM, 2, dtype=F32) / ROPE_AXIS_DIM)
    ang = jnp.concatenate([t_row[:, None] * inv_freq, t_col[:, None] * inv_freq], axis=-1)
    return jnp.cos(ang).astype(dtype).T, jnp.sin(ang).astype(dtype).T


def kernel(x, c, ctx, c_ctx, w_ada, b_ada, g_norm_mix, g_norm_ffn, w_in, b_gate, g_q, g_k, w_attn_proj,
           w_fourier_proj, w_o, w_gate_up, w_down, g_final):
    B, S, D = x.shape
    C = ctx.shape[1]
    assert D == D_MODEL and w_in.shape[0] == 1 and S % TM_PROJ == 0 and S % TQ == 0 and B < COND_ROWS
    layer = 0

    cond = jnp.zeros((COND_ROWS, D), F32).at[:B].set(c).at[B].set(c_ctx)
    mods = _adaln(cond, w_ada[layer], b_ada[layer])
    sh_m, sc_m, gt_m, sh_f, sc_f, gt_f = [m[:B, None, :] for m in jnp.split(mods, 6, axis=-1)]
    csh_m, csc_m = mods[B:B + 1, 0:D], mods[B:B + 1, D:2 * D]

    w_in_l = w_in[layer]
    wqkvT = w_in_l[:, :V_END].T.astype(BF16)
    wkvT = w_in_l[:, Q_W:V_END].T.astype(BF16)
    wf = w_in_l[:, V_END:F_END].astype(BF16)
    wg = w_in_l[:, F_END:].astype(BF16)
    gnorm_mix = g_norm_mix[layer].reshape(1, D)
    gnorm_ffn = g_norm_ffn[layer].reshape(1, D)

    gq_col = (g_q[layer] * (HEAD_DIM ** -0.5 * LOG2_E)).reshape(HEAD_DIM, 1)
    gk_col = g_k[layer].reshape(HEAD_DIM, 1)
    gq_b = jnp.broadcast_to(gq_col, (HEAD_DIM, TM_PROJ))
    gk_b = jnp.broadcast_to(gk_col, (HEAD_DIM, TM_PROJ))
    gk_ctx = jnp.broadcast_to(gk_col, (HEAD_DIM, C))
    cosT, sinT = _rope_tables_t(S, x.dtype)

    qT, k_lat, vT_lat, f = _in_proj(x, sh_m, sc_m, gnorm_mix, wqkvT, wf, gq_b, gk_b, cosT, sinT, 0)
    kc, vc = _ctx_kv(ctx, csh_m, csc_m, gnorm_mix, wkvT, gk_ctx)
    k_all = jnp.concatenate([k_lat, kc], axis=1)
    vT_all = jnp.concatenate([vT_lat, vc], axis=2)
    attn = _attention(qT, k_all, vT_all)
    four = _fourier(f)
    x1 = _merge(x, sh_m, sc_m, gt_m, gnorm_mix, wg, b_gate[layer].reshape(1, -1), attn, four,
                w_attn_proj[layer].astype(BF16), w_fourier_proj[layer].astype(BF16), w_o[layer].astype(BF16))
    return _ffn(x1, sh_f, sc_f, gt_f, gnorm_ffn, w_gate_up[layer].astype(BF16), w_down[layer].astype(BF16),
                g_final.reshape(1, D))
```

```python
import functools

import numpy as np
import jax
import jax.numpy as jnp
from jax import lax
from jax.experimental import pallas as pl
from jax.experimental.pallas import tpu as pltpu

D_MODEL = 1024
GRID_W = 64
HEAD_DIM = 64
N_HEADS = 16
N_KV_HEADS = 4
GROUP = N_HEADS // N_KV_HEADS
ROPE_AXIS_DIM = HEAD_DIM // 2
ROPE_THETA = 10000.0
N_FGROUPS = 4
FGROUP_DIM = 128
N_BRANCH = 2
Q_W = N_HEADS * HEAD_DIM
KV_W = N_KV_HEADS * HEAD_DIM
F_W = N_FGROUPS * FGROUP_DIM
K_END = Q_W + KV_W
V_END = K_END + KV_W
F_END = V_END + F_W
EPS = 1e-6
LOG2_E = 1.4426950408889634

BF16 = jnp.bfloat16
F32 = jnp.float32

VMEM_LIMIT_BYTES = 56 * 1024 * 1024

COND_ROWS = 16
ONES_ROWS = 16
TM_PROJ = 512
TM_FFN = 512
TQ = 512
KEY_CHUNK = 256
TR_FOURIER = 512


def _params(*sem):
    return pltpu.CompilerParams(dimension_semantics=sem, vmem_limit_bytes=VMEM_LIMIT_BYTES)


def _const_spec(shape):
    nd = len(shape)
    return pl.BlockSpec(shape, lambda *_: (0,) * nd, pipeline_mode=pl.Buffered(1))


def _sigmoid(x):
    return 1.0 / (1.0 + jnp.exp(-x))


def _modulated_norm(x, gain, shift, scale):
    ms = jnp.mean(x * x, axis=-1, keepdims=True)
    y = x * lax.rsqrt(ms + EPS) * gain
    return y * (1.0 + scale) + shift


def _head_norm_t(zh, gain):
    ms = jnp.mean(zh * zh, axis=0, keepdims=True)
    return zh * lax.rsqrt(ms + EPS) * gain


def _rope_t(zn, cos, sin):
    half = HEAD_DIM // 2
    x1, x2 = zn[:half], zn[half:]
    return x1 * cos - x2 * sin, x2 * cos + x1 * sin


def _adaln_kernel(cond_ref, w_ref, b_ref, o_ref):
    c = cond_ref[...]
    s = c * _sigmoid(c)
    o_ref[...] = jnp.dot(s, w_ref[...], preferred_element_type=F32,
                         precision=lax.Precision.HIGHEST) + b_ref[...]


def _adaln(cond, w, b):
    n = w.shape[1]
    tn = n // 4
    return pl.pallas_call(
        _adaln_kernel,
        grid=(n // tn,),
        in_specs=[_const_spec((COND_ROWS, D_MODEL)),
                  pl.BlockSpec((D_MODEL, tn), lambda j: (0, j)),
                  pl.BlockSpec((1, tn), lambda j: (0, j))],
        out_specs=pl.BlockSpec((COND_ROWS, tn), lambda j: (0, j)),
        out_shape=jax.ShapeDtypeStruct((COND_ROWS, n), F32),
        compiler_params=_params("parallel"),
        name="adaln",
    )(cond, w, b.reshape(1, n))


def _in_proj_kernel(x_ref, shift_ref, scale_ref, gnorm_ref, wqkvT_ref, wf_ref, gq_ref, gk_ref,
                    cos_ref, sin_ref, qT_ref, k_ref, vT_ref, f_ref):
    h = _modulated_norm(x_ref[0], gnorm_ref[...], shift_ref[0], scale_ref[0]).astype(BF16)
    zT = lax.dot_general(wqkvT_ref[...], h, (((1,), (1,)), ((), ())), preferred_element_type=F32)
    cos, sin = cos_ref[...], sin_ref[...]
    gq, gk = gq_ref[...], gk_ref[...]
    half = HEAD_DIM // 2
    for i in range(N_HEADS):
        lo = i * HEAD_DIM
        a, b = _rope_t(_head_norm_t(zT[lo:lo + HEAD_DIM], gq), cos, sin)
        qT_ref[0, lo:lo + half, :] = a.astype(BF16)
        qT_ref[0, lo + half:lo + HEAD_DIM, :] = b.astype(BF16)
    k_parts = []
    for i in range(N_KV_HEADS):
        lo = Q_W + i * HEAD_DIM
        k_parts.extend(_rope_t(_head_norm_t(zT[lo:lo + HEAD_DIM], gk), cos, sin))
    k_tok = jnp.concatenate(k_parts, axis=0).T
    for g in range(N_KV_HEADS):
        k_ref[0, g] = k_tok[:, g * HEAD_DIM:(g + 1) * HEAD_DIM].astype(BF16)
    vT_ref[0] = zT[K_END:V_END].astype(BF16)
    f_ref[0] = jnp.dot(h, wf_ref[...], preferred_element_type=F32).astype(BF16)


def _in_proj(x, shift, scale, gnorm, wqkvT, wf, gq_b, gk_b, cosT, sinT, n_ctx):
    B, S, D = x.shape
    tm = TM_PROJ
    half = HEAD_DIM // 2
    return pl.pallas_call(
        _in_proj_kernel,
        grid=(B, S // tm),
        in_specs=[pl.BlockSpec((1, tm, D), lambda b, i: (b, i, 0)),
                  pl.BlockSpec((1, 1, D), lambda b, i: (b, 0, 0)),
                  pl.BlockSpec((1, 1, D), lambda b, i: (b, 0, 0)),
                  _const_spec((1, D)),
                  _const_spec((V_END, D)), _const_spec((D, F_W)),
                  _const_spec((HEAD_DIM, tm)), _const_spec((HEAD_DIM, tm)),
                  pl.BlockSpec((half, tm), lambda b, i: (0, i)),
                  pl.BlockSpec((half, tm), lambda b, i: (0, i))],
        out_specs=[pl.BlockSpec((1, Q_W, tm), lambda b, i: (b, 0, i)),
                   pl.BlockSpec((1, N_KV_HEADS, tm, HEAD_DIM), lambda b, i: (b, 0, i, 0)),
                   pl.BlockSpec((1, KV_W, tm), lambda b, i: (b, 0, i)),
                   pl.BlockSpec((1, tm, F_W), lambda b, i: (b, i, 0))],
        out_shape=[jax.ShapeDtypeStruct((B, Q_W, S), BF16),
                   jax.ShapeDtypeStruct((B, N_KV_HEADS, S + n_ctx, HEAD_DIM), BF16),
                   jax.ShapeDtypeStruct((B, KV_W, S + n_ctx), BF16),
                   jax.ShapeDtypeStruct((B, S, F_W), BF16)],
        compiler_params=_params("parallel", "parallel"),
        name="in_proj",
    )(x, shift, scale, gnorm, wqkvT, wf, gq_b, gk_b, cosT, sinT)


def _ctx_kv_kernel(x_ref, shift_ref, scale_ref, gnorm_ref, wkvT_ref, gk_ref, k_in_ref, vT_in_ref,
                   k_ref, vT_ref):
    del k_in_ref, vT_in_ref
    h = _modulated_norm(x_ref[0], gnorm_ref[...], shift_ref[...], scale_ref[...]).astype(BF16)
    zT = lax.dot_general(wkvT_ref[...], h, (((1,), (1,)), ((), ())), preferred_element_type=F32)
    gk = gk_ref[...]
    kT = jnp.concatenate(
        [_head_norm_t(zT[i * HEAD_DIM:(i + 1) * HEAD_DIM], gk) for i in range(N_KV_HEADS)], axis=0)
    k_tok = kT.T
    for g in range(N_KV_HEADS):
        k_ref[0, g] = k_tok[:, g * HEAD_DIM:(g + 1) * HEAD_DIM].astype(BF16)
    vT_ref[0] = zT[KV_W:].astype(BF16)


def _ctx_kv(ctx, shift, scale, gnorm, wkvT, gk_b, k_all, vT_all):
    B, C, D = ctx.shape
    n_lat = k_all.shape[2] - C
    assert n_lat % C == 0
    blk = n_lat // C
    return pl.pallas_call(
        _ctx_kv_kernel,
        grid=(B,),
        in_specs=[pl.BlockSpec((1, C, D), lambda b: (b, 0, 0)),
                  _const_spec((1, D)), _const_spec((1, D)), _const_spec((1, D)),
                  _const_spec((2 * KV_W, D)), _const_spec((HEAD_DIM, C)),
                  pl.BlockSpec(memory_space=pl.ANY), pl.BlockSpec(memory_space=pl.ANY)],
        out_specs=[pl.BlockSpec((1, N_KV_HEADS, C, HEAD_DIM), lambda b: (b, 0, blk, 0)),
                   pl.BlockSpec((1, KV_W, C), lambda b: (b, 0, blk))],
        out_shape=[jax.ShapeDtypeStruct(k_all.shape, BF16),
                   jax.ShapeDtypeStruct(vT_all.shape, BF16)],
        input_output_aliases={6: 0, 7: 1},
        compiler_params=_params("parallel"),
        name="ctx_kv",
    )(ctx, shift, scale, gnorm, wkvT, gk_b, k_all, vT_all)


def _attention_kernel(qT_ref, k_ref, vT_ref, o_ref, vaug_ref, s0_ref, s1_ref, m0_ref, m1_ref, oT_ref):
    qi = pl.program_id(1)
    tq = qT_ref.shape[2]
    n_chunks = k_ref.shape[2] // KEY_CHUNK

    @pl.when(qi == 0)
    def _():
        ones = jnp.ones((ONES_ROWS, KEY_CHUNK), BF16)
        for g in range(N_KV_HEADS):
            for c in range(n_chunks):
                vaug_ref[g, c, :HEAD_DIM, :] = vT_ref[0, g * HEAD_DIM:(g + 1) * HEAD_DIM,
                                                      c * KEY_CHUNK:(c + 1) * KEY_CHUNK]
                vaug_ref[g, c, HEAD_DIM:, :] = ones

    def head_rows(t):
        return slice(t * HEAD_DIM, (t + 1) * HEAD_DIM)

    def step(t_b, sb, sa):
        m8, oa = jnp.full((8, tq), -jnp.inf, F32), None
        for c in range(n_chunks):
            rows = slice(c * KEY_CHUNK, (c + 1) * KEY_CHUNK)
            srows = pl.ds(pl.multiple_of(row0 + c * KEY_CHUNK, KEY_CHUNK), KEY_CHUNK)
            if sb is not None:
                p = jnp.exp2(sb[0][srows, :] - sb[1][0:1, :]).astype(BF16)
                d = jnp.dot(vaug_ref[t_b // GROUP, c], p, preferred_element_type=F32)
                oa = d if oa is None else oa + d
            if sa is not None:
                s = jnp.dot(k_ref[0, (t_b + 1) // GROUP, rows, :], qT_ref[0, head_rows(t_b + 1), :],
                            preferred_element_type=F32)
                sa[0][srows, :] = s
                m8 = jnp.maximum(m8, jnp.max(s.reshape(KEY_CHUNK // 8, 8, tq), axis=0))
        if sa is not None:
            sa[1][...] = jnp.broadcast_to(jnp.max(m8, axis=0, keepdims=True), (8, tq))
        if sb is not None:
            oT_ref[head_rows(t_b), :] = oa[:HEAD_DIM] / oa[HEAD_DIM:HEAD_DIM + 1]

    buf0, buf1 = (s0_ref, m0_ref), (s1_ref, m1_ref)
    row0 = jnp.minimum(pl.program_id(0), 0)
    step(-1, None, buf0)

    bufs = (buf0, buf1)
    for t in range(N_HEADS - 1):
        step(t, bufs[t % 2], bufs[(t + 1) % 2])
    step(N_HEADS - 1, bufs[(N_HEADS - 1) % 2], None)
    o_ref[0] = oT_ref[...].T.astype(BF16)


def _attention(qT, k_all, vT_all):
    B, _, S = qT.shape
    T = k_all.shape[2]
    assert T % KEY_CHUNK == 0
    n_chunks = T // KEY_CHUNK
    return pl.pallas_call(
        _attention_kernel,
        grid=(B, S // TQ),
        in_specs=[pl.BlockSpec((1, Q_W, TQ), lambda b, i: (b, 0, i)),
                  pl.BlockSpec((1, N_KV_HEADS, T, HEAD_DIM), lambda b, i: (b, 0, 0, 0)),
                  pl.BlockSpec((1, KV_W, T), lambda b, i: (b, 0, 0))],
        out_specs=pl.BlockSpec((1, TQ, Q_W), lambda b, i: (b, i, 0)),
        out_shape=jax.ShapeDtypeStruct((B, S, Q_W), BF16),
        scratch_shapes=[pltpu.VMEM((N_KV_HEADS, n_chunks, HEAD_DIM + ONES_ROWS, KEY_CHUNK), BF16),
                        pltpu.VMEM((T, TQ), F32), pltpu.VMEM((T, TQ), F32),
                        pltpu.VMEM((8, TQ), F32), pltpu.VMEM((8, TQ), F32),
                        pltpu.VMEM((Q_W, TQ), F32)],
        compiler_params=_params("parallel", "arbitrary"),
        name="attention",
    )(qT, k_all, vT_all)


@functools.lru_cache(maxsize=None)
def _dft_tables(n_pos):
    def cos_sin(n):
        idx = np.arange(n, dtype=np.int64)
        ang = 2.0 * np.pi * ((idx[:, None] * idx[None, :]) % n).astype(np.float64) / n
        return np.cos(ang), np.sin(ang)
    cc, sc = cos_sin(FGROUP_DIM)
    cl, sl = cos_sin(n_pos)
    chan = np.concatenate([cc, sc], axis=1).astype(np.float32)
    pos = np.concatenate([cl, -sl], axis=1).astype(np.float32)
    return chan, pos


def _fourier_kernel(f_ref, chan_ref, pos_ref, o_ref, z_ref, *, scale):
    n_pos = f_ref.shape[1]

    @pl.when(pl.program_id(1) == 0)
    def _():
        for g in range(N_FGROUPS):
            cols = slice(g * FGROUP_DIM, (g + 1) * FGROUP_DIM)
            xcs = jnp.dot(f_ref[0, :, cols], chan_ref[...], preferred_element_type=F32)
            z_ref[0:n_pos, cols] = xcs[:, :FGROUP_DIM].astype(BF16)
            z_ref[n_pos:2 * n_pos, cols] = xcs[:, FGROUP_DIM:].astype(BF16)

    y = jnp.dot(pos_ref[...], z_ref[...], preferred_element_type=F32)
    o_ref[0] = (y * scale).astype(BF16)


def _fourier(f):
    B, S, W = f.shape
    chan, pos = _dft_tables(S)
    tr = TR_FOURIER
    scale = float(1.0 / np.sqrt(S * FGROUP_DIM))
    return pl.pallas_call(
        functools.partial(_fourier_kernel, scale=scale),
        grid=(B, S // tr),
        in_specs=[pl.BlockSpec((1, S, W), lambda b, i: (b, 0, 0)),
                  _const_spec((FGROUP_DIM, 2 * FGROUP_DIM)),
                  pl.BlockSpec((tr, 2 * S), lambda b, i: (i, 0))],
        out_specs=pl.BlockSpec((1, tr, W), lambda b, i: (b, i, 0)),
        out_shape=jax.ShapeDtypeStruct((B, S, W), BF16),
        scratch_shapes=[pltpu.VMEM((2 * S, W), BF16)],
        compiler_params=_params("parallel", "arbitrary"),
        name="fourier",
    )(f, jnp.asarray(chan).astype(BF16), jnp.asarray(pos).astype(BF16))


def _merge_kernel(x_ref, shift_ref, scale_ref, gate_ref, gnorm_ref, wg_ref, bg_ref, attn_ref, four_ref,
                  wap_ref, wfp_ref, wo_ref, o_ref):
    x = x_ref[0]
    h = _modulated_norm(x, gnorm_ref[...], shift_ref[0], scale_ref[0]).astype(BF16)
    gates = _sigmoid(jnp.dot(h, wg_ref[...], preferred_element_type=F32) + bg_ref[...])
    a = jnp.dot(attn_ref[0], wap_ref[...], preferred_element_type=F32)
    fo = jnp.dot(four_ref[0], wfp_ref[...], preferred_element_type=F32)
    merged = gates[:, :D_MODEL] * a + gates[:, D_MODEL:] * fo
    mix = jnp.dot(merged.astype(BF16), wo_ref[...], preferred_element_type=F32)
    o_ref[0] = x + gate_ref[0] * mix


def _merge(x, shift, scale, gate, gnorm, wg, bg, attn, four, wap, wfp, wo):
    B, S, D = x.shape
    tm = TM_PROJ
    vec = pl.BlockSpec((1, 1, D), lambda b, i: (b, 0, 0))
    return pl.pallas_call(
        _merge_kernel,
        grid=(B, S // tm),
        in_specs=[pl.BlockSpec((1, tm, D), lambda b, i: (b, i, 0)), vec, vec, vec,
                  _const_spec((1, D)),
                  _const_spec((D, N_BRANCH * D)), _const_spec((1, N_BRANCH * D)),
                  pl.BlockSpec((1, tm, Q_W), lambda b, i: (b, i, 0)),
                  pl.BlockSpec((1, tm, F_W), lambda b, i: (b, i, 0)),
                  _const_spec((Q_W, D)), _const_spec((F_W, D)), _const_spec((D, D))],
        out_specs=pl.BlockSpec((1, tm, D), lambda b, i: (b, i, 0)),
        out_shape=jax.ShapeDtypeStruct((B, S, D), F32),
        compiler_params=_params("parallel", "parallel"),
        name="merge",
    )(x, shift, scale, gate, gnorm, wg, bg, attn, four, wap, wfp, wo)


def _ffn_kernel(x_ref, shift_ref, scale_ref, gate_ref, gnorm_ref, wgu_ref, wdown_ref, gfinal_ref, o_ref):
    x = x_ref[0]
    d_ff = wdown_ref.shape[0]
    h = _modulated_norm(x, gnorm_ref[...], shift_ref[0], scale_ref[0]).astype(BF16)
    gu = jnp.dot(h, wgu_ref[...], preferred_element_type=F32)
    gate, up = gu[:, :d_ff], gu[:, d_ff:]
    act = (gate * _sigmoid(gate) * up).astype(BF16)
    x2 = x + gate_ref[0] * jnp.dot(act, wdown_ref[...], preferred_element_type=F32)
    ms = jnp.mean(x2 * x2, axis=-1, keepdims=True)
    o_ref[0] = x2 * lax.rsqrt(ms + EPS) * gfinal_ref[...]


def _ffn(x, shift, scale, gate, gnorm, wgu, wdown, gfinal):
    B, S, D = x.shape
    d_ff = wdown.shape[0]
    tm = TM_FFN
    vec = pl.BlockSpec((1, 1, D), lambda b, i: (b, 0, 0))
    return pl.pallas_call(
        _ffn_kernel,
        grid=(B, S // tm),
        in_specs=[pl.BlockSpec((1, tm, D), lambda b, i: (b, i, 0)), vec, vec, vec,
                  _const_spec((1, D)),
                  _const_spec((D, 2 * d_ff)), _const_spec((d_ff, D)), _const_spec((1, D))],
        out_specs=pl.BlockSpec((1, tm, D), lambda b, i: (b, i, 0)),
        out_shape=jax.ShapeDtypeStruct((B, S, D), F32),
        compiler_params=_params("parallel", "parallel"),
        name="ffn",
    )(x, shift, scale, gate, gnorm, wgu, wdown, gfinal)


def _rope_tables_t(seq, dtype):
    rows = seq // GRID_W
    t_row = jnp.repeat(jnp.arange(rows, dtype=F32), GRID_W)
    t_col = jnp.tile(jnp.arange(GRID_W, dtype=F32), rows)
    inv_freq = ROPE_THETA ** (-jnp.arange(0, ROPE_AXIS_DIM, 2, dtype=F32) / ROPE_AXIS_DIM)
    ang = jnp.concatenate([t_row[:, None] * inv_freq, t_col[:, None] * inv_freq], axis=-1)
    return jnp.cos(ang).astype(dtype).T, jnp.sin(ang).astype(dtype).T


def kernel(x, c, ctx, c_ctx, w_ada, b_ada, g_norm_mix, g_norm_ffn, w_in, b_gate, g_q, g_k, w_attn_proj,
           w_fourier_proj, w_o, w_gate_up, w_down, g_final):
    B, S, D = x.shape
    C = ctx.shape[1]
    assert D == D_MODEL and w_in.shape[0] == 1 and S % TM_PROJ == 0 and S % TQ == 0 and B < COND_ROWS
    layer = 0

    cond = jnp.zeros((COND_ROWS, D), F32).at[:B].set(c).at[B].set(c_ctx)
    mods = _adaln(cond, w_ada[layer], b_ada[layer])
    sh_m, sc_m, gt_m, sh_f, sc_f, gt_f = [m[:B, None, :] for m in jnp.split(mods, 6, axis=-1)]
    csh_m, csc_m = mods[B:B + 1, 0:D], mods[B:B + 1, D:2 * D]

    w_in_l = w_in[layer]
    wqkvT = w_in_l[:, :V_END].T.astype(BF16)
    wkvT = w_in_l[:, Q_W:V_END].T.astype(BF16)
    wf = w_in_l[:, V_END:F_END].astype(BF16)
    wg = w_in_l[:, F_END:].astype(BF16)
    gnorm_mix = g_norm_mix[layer].reshape(1, D)
    gnorm_ffn = g_norm_ffn[layer].reshape(1, D)

    gq_col = (g_q[layer] * (HEAD_DIM ** -0.5 * LOG2_E)).reshape(HEAD_DIM, 1)
    gk_col = g_k[layer].reshape(HEAD_DIM, 1)
    gq_b = jnp.broadcast_to(gq_col, (HEAD_DIM, TM_PROJ))
    gk_b = jnp.broadcast_to(gk_col, (HEAD_DIM, TM_PROJ))
    gk_ctx = jnp.broadcast_to(gk_col, (HEAD_DIM, C))
    cosT, sinT = _rope_tables_t(S, x.dtype)

    qT, k_all, vT_all, f = _in_proj(x, sh_m, sc_m, gnorm_mix, wqkvT, wf, gq_b, gk_b, cosT, sinT, C)
    k_all, vT_all = _ctx_kv(ctx, csh_m, csc_m, gnorm_mix, wkvT, gk_ctx, k_all, vT_all)
    attn = _attention(qT, k_all, vT_all)
    four = _fourier(f)
    x1 = _merge(x, sh_m, sc_m, gt_m, gnorm_mix, wg, b_gate[layer].reshape(1, -1), attn, four,
                w_attn_proj[layer].astype(BF16), w_fourier_proj[layer].astype(BF16), w_o[layer].astype(BF16))
    return _ffn(x1, sh_f, sc_f, gt_f, gnorm_ffn, w_gate_up[layer].astype(BF16), w_down[layer].astype(BF16),
                g_final.reshape(1, D))
```

```python
import functools

import numpy as np
import jax
import jax.numpy as jnp
from jax import lax
from jax.experimental import pallas as pl
from jax.experimental.pallas import tpu as pltpu

D_MODEL = 1024
GRID_W = 64
HEAD_DIM = 64
N_HEADS = 16
N_KV_HEADS = 4
GROUP = N_HEADS // N_KV_HEADS
ROPE_AXIS_DIM = HEAD_DIM // 2
ROPE_THETA = 10000.0
N_FGROUPS = 4
FGROUP_DIM = 128
N_BRANCH = 2
Q_W = N_HEADS * HEAD_DIM
KV_W = N_KV_HEADS * HEAD_DIM
F_W = N_FGROUPS * FGROUP_DIM
K_END = Q_W + KV_W
V_END = K_END + KV_W
F_END = V_END + F_W
EPS = 1e-6
LOG2_E = 1.4426950408889634

BF16 = jnp.bfloat16
F32 = jnp.float32

VMEM_LIMIT_BYTES = 56 * 1024 * 1024

COND_ROWS = 16
ONES_ROWS = 16
TM_PROJ = 512
TM_MERGE = 1024
TM_FFN = 1024
FF_CHUNK = 256
TQ = 512
KEY_CHUNK = 256


def _params(*sem):
    return pltpu.CompilerParams(dimension_semantics=sem, vmem_limit_bytes=VMEM_LIMIT_BYTES)


def _const_spec(shape):
    nd = len(shape)
    return pl.BlockSpec(shape, lambda *_: (0,) * nd, pipeline_mode=pl.Buffered(1))


def _sigmoid(x):
    return 1.0 / (1.0 + jnp.exp(-x))


def _modulated_norm(x, gain, shift, scale):
    ms = jnp.mean(x * x, axis=-1, keepdims=True)
    y = x * lax.rsqrt(ms + EPS) * gain
    return y * (1.0 + scale) + shift


def _head_norm_t(zh, gain):
    ms = jnp.mean(zh * zh, axis=0, keepdims=True)
    return zh * lax.rsqrt(ms + EPS) * gain


def _rope_t(zn, cos, sin):
    half = HEAD_DIM // 2
    x1, x2 = zn[:half], zn[half:]
    return x1 * cos - x2 * sin, x2 * cos + x1 * sin


def _adaln_kernel(cond_ref, w_ref, b_ref, o_ref):
    c = cond_ref[...]
    s = c * _sigmoid(c)
    o_ref[...] = jnp.dot(s, w_ref[...], preferred_element_type=F32,
                         precision=lax.Precision.HIGHEST) + b_ref[...]


def _adaln(cond, w, b):
    n = w.shape[1]
    tn = n // 4
    return pl.pallas_call(
        _adaln_kernel,
        grid=(n // tn,),
        in_specs=[_const_spec((COND_ROWS, D_MODEL)),
                  pl.BlockSpec((D_MODEL, tn), lambda j: (0, j)),
                  pl.BlockSpec((1, tn), lambda j: (0, j))],
        out_specs=pl.BlockSpec((COND_ROWS, tn), lambda j: (0, j)),
        out_shape=jax.ShapeDtypeStruct((COND_ROWS, n), F32),
        compiler_params=_params("parallel"),
        name="adaln",
    )(cond, w, b.reshape(1, n))


def _in_proj_kernel(x_ref, shift_ref, scale_ref, gnorm_ref, wqkvT_ref, wf_ref, gq_ref, gk_ref,
                    cos_ref, sin_ref, qT_ref, k_ref, vT_ref, f_ref):
    h = _modulated_norm(x_ref[0], gnorm_ref[...], shift_ref[0], scale_ref[0]).astype(BF16)
    zT = lax.dot_general(wqkvT_ref[...], h, (((1,), (1,)), ((), ())), preferred_element_type=F32)
    cos, sin = cos_ref[...], sin_ref[...]
    gq, gk = gq_ref[...], gk_ref[...]
    half = HEAD_DIM // 2
    for i in range(N_HEADS):
        lo = i * HEAD_DIM
        a, b = _rope_t(_head_norm_t(zT[lo:lo + HEAD_DIM], gq), cos, sin)
        qT_ref[0, lo:lo + half, :] = a.astype(BF16)
        qT_ref[0, lo + half:lo + HEAD_DIM, :] = b.astype(BF16)
    k_parts = []
    for i in range(N_KV_HEADS):
        lo = Q_W + i * HEAD_DIM
        k_parts.extend(_rope_t(_head_norm_t(zT[lo:lo + HEAD_DIM], gk), cos, sin))
    k_tok = jnp.concatenate(k_parts, axis=0).T
    for g in range(N_KV_HEADS):
        k_ref[0, g] = k_tok[:, g * HEAD_DIM:(g + 1) * HEAD_DIM].astype(BF16)
    vT_ref[0] = zT[K_END:V_END].astype(BF16)
    f_ref[0] = jnp.dot(h, wf_ref[...], preferred_element_type=F32).astype(BF16)


def _in_proj(x, shift, scale, gnorm, wqkvT, wf, gq_b, gk_b, cosT, sinT, n_ctx):
    B, S, D = x.shape
    tm = TM_PROJ
    half = HEAD_DIM // 2
    return pl.pallas_call(
        _in_proj_kernel,
        grid=(B, S // tm),
        in_specs=[pl.BlockSpec((1, tm, D), lambda b, i: (b, i, 0)),
                  pl.BlockSpec((1, 1, D), lambda b, i: (b, 0, 0)),
                  pl.BlockSpec((1, 1, D), lambda b, i: (b, 0, 0)),
                  _const_spec((1, D)),
                  _const_spec((V_END, D)), _const_spec((D, F_W)),
                  _const_spec((HEAD_DIM, tm)), _const_spec((HEAD_DIM, tm)),
                  pl.BlockSpec((half, tm), lambda b, i: (0, i)),
                  pl.BlockSpec((half, tm), lambda b, i: (0, i))],
        out_specs=[pl.BlockSpec((1, Q_W, tm), lambda b, i: (b, 0, i)),
                   pl.BlockSpec((1, N_KV_HEADS, tm, HEAD_DIM), lambda b, i: (b, 0, i, 0)),
                   pl.BlockSpec((1, KV_W, tm), lambda b, i: (b, 0, i)),
                   pl.BlockSpec((1, tm, F_W), lambda b, i: (b, i, 0))],
        out_shape=[jax.ShapeDtypeStruct((B, Q_W, S), BF16),
                   jax.ShapeDtypeStruct((B, N_KV_HEADS, S + n_ctx, HEAD_DIM), BF16),
                   jax.ShapeDtypeStruct((B, KV_W, S + n_ctx), BF16),
                   jax.ShapeDtypeStruct((B, S, F_W), BF16)],
        compiler_params=_params("parallel", "parallel"),
        name="in_proj",
    )(x, shift, scale, gnorm, wqkvT, wf, gq_b, gk_b, cosT, sinT)


def _ctx_kv_kernel(x_ref, shift_ref, scale_ref, gnorm_ref, wkvT_ref, gk_ref, k_in_ref, vT_in_ref,
                   k_ref, vT_ref):
    del k_in_ref, vT_in_ref
    h = _modulated_norm(x_ref[0], gnorm_ref[...], shift_ref[...], scale_ref[...]).astype(BF16)
    zT = lax.dot_general(wkvT_ref[...], h, (((1,), (1,)), ((), ())), preferred_element_type=F32)
    gk = gk_ref[...]
    kT = jnp.concatenate(
        [_head_norm_t(zT[i * HEAD_DIM:(i + 1) * HEAD_DIM], gk) for i in range(N_KV_HEADS)], axis=0)
    k_tok = kT.T
    for g in range(N_KV_HEADS):
        k_ref[0, g] = k_tok[:, g * HEAD_DIM:(g + 1) * HEAD_DIM].astype(BF16)
    vT_ref[0] = zT[KV_W:].astype(BF16)


def _ctx_kv(ctx, shift, scale, gnorm, wkvT, gk_b, k_all, vT_all):
    B, C, D = ctx.shape
    n_lat = k_all.shape[2] - C
    assert n_lat % C == 0
    blk = n_lat // C
    return pl.pallas_call(
        _ctx_kv_kernel,
        grid=(B,),
        in_specs=[pl.BlockSpec((1, C, D), lambda b: (b, 0, 0)),
                  _const_spec((1, D)), _const_spec((1, D)), _const_spec((1, D)),
                  _const_spec((2 * KV_W, D)), _const_spec((HEAD_DIM, C)),
                  pl.BlockSpec(memory_space=pl.ANY), pl.BlockSpec(memory_space=pl.ANY)],
        out_specs=[pl.BlockSpec((1, N_KV_HEADS, C, HEAD_DIM), lambda b: (b, 0, blk, 0)),
                   pl.BlockSpec((1, KV_W, C), lambda b: (b, 0, blk))],
        out_shape=[jax.ShapeDtypeStruct(k_all.shape, BF16),
                   jax.ShapeDtypeStruct(vT_all.shape, BF16)],
        input_output_aliases={6: 0, 7: 1},
        compiler_params=_params("parallel"),
        name="ctx_kv",
    )(ctx, shift, scale, gnorm, wkvT, gk_b, k_all, vT_all)


def _attention_kernel(qT_ref, k_ref, vT_ref, o_ref, vaug_ref, s0_ref, s1_ref, m0_ref, m1_ref, oT_ref):
    qi = pl.program_id(1)
    tq = qT_ref.shape[2]
    n_chunks = k_ref.shape[2] // KEY_CHUNK

    @pl.when(qi == 0)
    def _():
        ones = jnp.ones((ONES_ROWS, KEY_CHUNK), BF16)
        for g in range(N_KV_HEADS):
            for c in range(n_chunks):
                vaug_ref[g, c, :HEAD_DIM, :] = vT_ref[0, g * HEAD_DIM:(g + 1) * HEAD_DIM,
                                                      c * KEY_CHUNK:(c + 1) * KEY_CHUNK]
                vaug_ref[g, c, HEAD_DIM:, :] = ones

    def head_rows(t):
        return slice(t * HEAD_DIM, (t + 1) * HEAD_DIM)

    def step(t_b, sb, sa):
        m8, oa = jnp.full((8, tq), -jnp.inf, F32), None
        for c in range(n_chunks):
            rows = slice(c * KEY_CHUNK, (c + 1) * KEY_CHUNK)
            srows = pl.ds(pl.multiple_of(row0 + c * KEY_CHUNK, KEY_CHUNK), KEY_CHUNK)
            if sb is not None:
                p = jnp.exp2(sb[0][srows, :] - sb[1][0:1, :]).astype(BF16)
                d = jnp.dot(vaug_ref[t_b // GROUP, c], p, preferred_element_type=F32)
                oa = d if oa is None else oa + d
            if sa is not None:
                s = jnp.dot(k_ref[0, (t_b + 1) // GROUP, rows, :], qT_ref[0, head_rows(t_b + 1), :],
                            preferred_element_type=F32)
                sa[0][srows, :] = s
                m8 = jnp.maximum(m8, jnp.max(s.reshape(KEY_CHUNK // 8, 8, tq), axis=0))
        if sa is not None:
            sa[1][...] = jnp.broadcast_to(jnp.max(m8, axis=0, keepdims=True), (8, tq))
        if sb is not None:
            oT_ref[head_rows(t_b), :] = oa[:HEAD_DIM] / oa[HEAD_DIM:HEAD_DIM + 1]

    buf0, buf1 = (s0_ref, m0_ref), (s1_ref, m1_ref)
    row0 = jnp.minimum(pl.program_id(0), 0)
    step(-1, None, buf0)

    bufs = (buf0, buf1)
    for t in range(N_HEADS - 1):
        step(t, bufs[t % 2], bufs[(t + 1) % 2])
    step(N_HEADS - 1, bufs[(N_HEADS - 1) % 2], None)
    o_ref[0] = oT_ref[...].T.astype(BF16)


def _attention(qT, k_all, vT_all):
    B, _, S = qT.shape
    T = k_all.shape[2]
    assert T % KEY_CHUNK == 0
    n_chunks = T // KEY_CHUNK
    return pl.pallas_call(
        _attention_kernel,
        grid=(B, S // TQ),
        in_specs=[pl.BlockSpec((1, Q_W, TQ), lambda b, i: (b, 0, i)),
                  pl.BlockSpec((1, N_KV_HEADS, T, HEAD_DIM), lambda b, i: (b, 0, 0, 0)),
                  pl.BlockSpec((1, KV_W, T), lambda b, i: (b, 0, 0))],
        out_specs=pl.BlockSpec((1, TQ, Q_W), lambda b, i: (b, i, 0)),
        out_shape=jax.ShapeDtypeStruct((B, S, Q_W), BF16),
        scratch_shapes=[pltpu.VMEM((N_KV_HEADS, n_chunks, HEAD_DIM + ONES_ROWS, KEY_CHUNK), BF16),
                        pltpu.VMEM((T, TQ), F32), pltpu.VMEM((T, TQ), F32),
                        pltpu.VMEM((8, TQ), F32), pltpu.VMEM((8, TQ), F32),
                        pltpu.VMEM((Q_W, TQ), F32)],
        compiler_params=_params("parallel", "arbitrary"),
        name="attention",
    )(qT, k_all, vT_all)


@functools.lru_cache(maxsize=None)
def _dft_tables(n_pos):
    def cos_sin(n):
        idx = np.arange(n, dtype=np.int64)
        ang = 2.0 * np.pi * ((idx[:, None] * idx[None, :]) % n).astype(np.float64) / n
        return np.cos(ang), np.sin(ang)
    cc, sc = cos_sin(FGROUP_DIM)
    cl, sl = cos_sin(n_pos)
    chan = np.concatenate([cc, sc], axis=1).astype(np.float32)
    pos = np.concatenate([cl, -sl], axis=1).astype(np.float32)
    return chan, pos


def _fourier_kernel(f_ref, chan_ref, pos_ref, o_ref, z_ref, *, scale):
    n_pos = f_ref.shape[1]
    for g in range(N_FGROUPS):
        cols = slice(g * FGROUP_DIM, (g + 1) * FGROUP_DIM)
        xcs = jnp.dot(f_ref[0, :, cols], chan_ref[...], preferred_element_type=F32)
        z_ref[0:n_pos, cols] = xcs[:, :FGROUP_DIM].astype(BF16)
        z_ref[n_pos:2 * n_pos, cols] = xcs[:, FGROUP_DIM:].astype(BF16)
    y = jnp.dot(pos_ref[...], z_ref[...], preferred_element_type=F32)
    o_ref[0] = (y * scale).astype(BF16)


def _fourier(f):
    B, S, W = f.shape
    chan, pos = _dft_tables(S)
    scale = float(1.0 / np.sqrt(S * FGROUP_DIM))
    return pl.pallas_call(
        functools.partial(_fourier_kernel, scale=scale),
        grid=(B,),
        in_specs=[pl.BlockSpec((1, S, W), lambda b: (b, 0, 0)),
                  _const_spec((FGROUP_DIM, 2 * FGROUP_DIM)),
                  _const_spec((S, 2 * S))],
        out_specs=pl.BlockSpec((1, S, W), lambda b: (b, 0, 0)),
        out_shape=jax.ShapeDtypeStruct((B, S, W), BF16),
        scratch_shapes=[pltpu.VMEM((2 * S, W), BF16)],
        compiler_params=_params("parallel"),
        name="fourier",
    )(f, jnp.asarray(chan).astype(BF16), jnp.asarray(pos).astype(BF16))


def _merge_kernel(x_ref, shift_ref, scale_ref, gate_ref, gnorm_ref, wg_ref, bg_ref, attn_ref, four_ref,
                  wap_ref, wfp_ref, wo_ref, o_ref):
    x = x_ref[0]
    h = _modulated_norm(x, gnorm_ref[...], shift_ref[0], scale_ref[0]).astype(BF16)
    gates = _sigmoid(jnp.dot(h, wg_ref[...], preferred_element_type=F32) + bg_ref[...])
    a = jnp.dot(attn_ref[0], wap_ref[...], preferred_element_type=F32)
    fo = jnp.dot(four_ref[0], wfp_ref[...], preferred_element_type=F32)
    merged = gates[:, :D_MODEL] * a + gates[:, D_MODEL:] * fo
    mix = jnp.dot(merged.astype(BF16), wo_ref[...], preferred_element_type=F32)
    o_ref[0] = x + gate_ref[0] * mix


def _merge(x, shift, scale, gate, gnorm, wg, bg, attn, four, wap, wfp, wo):
    B, S, D = x.shape
    tm = TM_MERGE
    vec = pl.BlockSpec((1, 1, D), lambda b, i: (b, 0, 0))
    return pl.pallas_call(
        _merge_kernel,
        grid=(B, S // tm),
        in_specs=[pl.BlockSpec((1, tm, D), lambda b, i: (b, i, 0)), vec, vec, vec,
                  _const_spec((1, D)),
                  _const_spec((D, N_BRANCH * D)), _const_spec((1, N_BRANCH * D)),
                  pl.BlockSpec((1, tm, Q_W), lambda b, i: (b, i, 0)),
                  pl.BlockSpec((1, tm, F_W), lambda b, i: (b, i, 0)),
                  _const_spec((Q_W, D)), _const_spec((F_W, D)), _const_spec((D, D))],
        out_specs=pl.BlockSpec((1, tm, D), lambda b, i: (b, i, 0)),
        out_shape=jax.ShapeDtypeStruct((B, S, D), F32),
        compiler_params=_params("parallel", "parallel"),
        name="merge",
    )(x, shift, scale, gate, gnorm, wg, bg, attn, four, wap, wfp, wo)


def _ffn_kernel(x_ref, shift_ref, scale_ref, gate_ref, gnorm_ref, wgu_ref, wdown_ref, gfinal_ref, o_ref):
    x = x_ref[0]
    d_ff = wdown_ref.shape[0]
    h = _modulated_norm(x, gnorm_ref[...], shift_ref[0], scale_ref[0]).astype(BF16)
    ffn = None
    for lo in range(0, d_ff, FF_CHUNK):
        gate = jnp.dot(h, wgu_ref[:, lo:lo + FF_CHUNK], preferred_element_type=F32)
        up = jnp.dot(h, wgu_ref[:, d_ff + lo:d_ff + lo + FF_CHUNK], preferred_element_type=F32)
        act = (gate * _sigmoid(gate) * up).astype(BF16)
        part = jnp.dot(act, wdown_ref[lo:lo + FF_CHUNK, :], preferred_element_type=F32)
        ffn = part if ffn is None else ffn + part
    x2 = x + gate_ref[0] * ffn
    ms = jnp.mean(x2 * x2, axis=-1, keepdims=True)
    o_ref[0] = x2 * lax.rsqrt(ms + EPS) * gfinal_ref[...]


def _ffn(x, shift, scale, gate, gnorm, wgu, wdown, gfinal):
    B, S, D = x.shape
    d_ff = wdown.shape[0]
    tm = TM_FFN
    vec = pl.BlockSpec((1, 1, D), lambda b, i: (b, 0, 0))
    return pl.pallas_call(
        _ffn_kernel,
        grid=(B, S // tm),
        in_specs=[pl.BlockSpec((1, tm, D), lambda b, i: (b, i, 0)), vec, vec, vec,
                  _const_spec((1, D)),
                  _const_spec((D, 2 * d_ff)), _const_spec((d_ff, D)), _const_spec((1, D))],
        out_specs=pl.BlockSpec((1, tm, D), lambda b, i: (b, i, 0)),
        out_shape=jax.ShapeDtypeStruct((B, S, D), F32),
        compiler_params=_params("parallel", "parallel"),
        name="ffn",
    )(x, shift, scale, gate, gnorm, wgu, wdown, gfinal)


def _rope_tables_t(seq, dtype):
    rows = seq // GRID_W
    t_row = jnp.repeat(jnp.arange(rows, dtype=F32), GRID_W)
    t_col = jnp.tile(jnp.arange(GRID_W, dtype=F32), rows)
    inv_freq = ROPE_THETA ** (-jnp.arange(0, ROPE_AXIS_DIM, 2, dtype=F32) / ROPE_AXIS_DIM)
    ang = jnp.concatenate([t_row[:, None] * inv_freq, t_col[:, None] * inv_freq], axis=-1)
    return jnp.cos(ang).astype(dtype).T, jnp.sin(ang).astype(dtype).T


def kernel(x, c, ctx, c_ctx, w_ada, b_ada, g_norm_mix, g_norm_ffn, w_in, b_gate, g_q, g_k, w_attn_proj,
           w_fourier_proj, w_o, w_gate_up, w_down, g_final):
    B, S, D = x.shape
    C = ctx.shape[1]
    assert D == D_MODEL and w_in.shape[0] == 1 and B < COND_ROWS
    assert S % TM_PROJ == 0 and S % TM_MERGE == 0 and S % TM_FFN == 0 and S % TQ == 0
    assert w_down.shape[1] % FF_CHUNK == 0
    layer = 0

    cond = jnp.zeros((COND_ROWS, D), F32).at[:B].set(c).at[B].set(c_ctx)
    mods = _adaln(cond, w_ada[layer], b_ada[layer])
    sh_m, sc_m, gt_m, sh_f, sc_f, gt_f = [m[:B, None, :] for m in jnp.split(mods, 6, axis=-1)]
    csh_m, csc_m = mods[B:B + 1, 0:D], mods[B:B + 1, D:2 * D]

    w_in_l = w_in[layer]
    wqkvT = w_in_l[:, :V_END].T.astype(BF16)
    wkvT = w_in_l[:, Q_W:V_END].T.astype(BF16)
    wf = w_in_l[:, V_END:F_END].astype(BF16)
    wg = w_in_l[:, F_END:].astype(BF16)
    gnorm_mix = g_norm_mix[layer].reshape(1, D)
    gnorm_ffn = g_norm_ffn[layer].reshape(1, D)

    gq_col = (g_q[layer] * (HEAD_DIM ** -0.5 * LOG2_E)).reshape(HEAD_DIM, 1)
    gk_col = g_k[layer].reshape(HEAD_DIM, 1)
    gq_b = jnp.broadcast_to(gq_col, (HEAD_DIM, TM_PROJ))
    gk_b = jnp.broadcast_to(gk_col, (HEAD_DIM, TM_PROJ))
    gk_ctx = jnp.broadcast_to(gk_col, (HEAD_DIM, C))
    cosT, sinT = _rope_tables_t(S, x.dtype)

    qT, k_all, vT_all, f = _in_proj(x, sh_m, sc_m, gnorm_mix, wqkvT, wf, gq_b, gk_b, cosT, sinT, C)
    k_all, vT_all = _ctx_kv(ctx, csh_m, csc_m, gnorm_mix, wkvT, gk_ctx, k_all, vT_all)
    attn = _attention(qT, k_all, vT_all)
    four = _fourier(f)
    x1 = _merge(x, sh_m, sc_m, gt_m, gnorm_mix, wg, b_gate[layer].reshape(1, -1), attn, four,
                w_attn_proj[layer].astype(BF16), w_fourier_proj[layer].astype(BF16), w_o[layer].astype(BF16))
    return _ffn(x1, sh_f, sc_f, gt_f, gnorm_ffn, w_gate_up[layer].astype(BF16), w_down[layer].astype(BF16),
                g_final.reshape(1, D))
```

```python
import functools

import numpy as np
import jax
import jax.numpy as jnp
from jax import lax
from jax.experimental import pallas as pl
from jax.experimental.pallas import tpu as pltpu

D_MODEL = 1024
GRID_W = 64
HEAD_DIM = 64
N_HEADS = 16
N_KV_HEADS = 4
GROUP = N_HEADS // N_KV_HEADS
ROPE_AXIS_DIM = HEAD_DIM // 2
ROPE_THETA = 10000.0
N_FGROUPS = 4
FGROUP_DIM = 128
N_BRANCH = 2
Q_W = N_HEADS * HEAD_DIM
KV_W = N_KV_HEADS * HEAD_DIM
F_W = N_FGROUPS * FGROUP_DIM
K_END = Q_W + KV_W
V_END = K_END + KV_W
F_END = V_END + F_W
EPS = 1e-6
LOG2_E = 1.4426950408889634

BF16 = jnp.bfloat16
F32 = jnp.float32

VMEM_LIMIT_BYTES = 56 * 1024 * 1024

COND_ROWS = 16
N_MODS = 6
ONES_ROWS = 16
TM_PROJ = 2048
TM_SUB = 512
TM_MERGE = 1024
TM_FFN = 1024
FF_CHUNK = 256
TQ = 512
KEY_CHUNK = 256


def _params(*sem):
    return pltpu.CompilerParams(dimension_semantics=sem, vmem_limit_bytes=VMEM_LIMIT_BYTES)


def _const_spec(shape):
    nd = len(shape)
    return pl.BlockSpec(shape, lambda *_: (0,) * nd, pipeline_mode=pl.Buffered(1))


def _mod_spec(k, row=None):
    if row is None:
        return pl.BlockSpec((1, 1, D_MODEL), lambda b, *_: (b * N_MODS + k, 0, 0))
    return pl.BlockSpec((1, 1, D_MODEL), lambda *_: (row * N_MODS + k, 0, 0))


def _sigmoid(x):
    return 1.0 / (1.0 + jnp.exp(-x))


def _modulated_norm(x, gain, shift, scale):
    ms = jnp.mean(x * x, axis=-1, keepdims=True)
    y = x * lax.rsqrt(ms + EPS) * gain
    return y * (1.0 + scale) + shift


def _head_norm_t(zh, gain):
    ms = jnp.mean(zh * zh, axis=0, keepdims=True)
    return zh * lax.rsqrt(ms + EPS) * gain


def _rope_t(zn, cos, sin):
    half = HEAD_DIM // 2
    x1, x2 = zn[:half], zn[half:]
    return x1 * cos - x2 * sin, x2 * cos + x1 * sin


def _adaln_kernel(cond_ref, w_ref, b_ref, o_ref):
    c = cond_ref[...]
    s = c * _sigmoid(c)
    o_ref[...] = jnp.dot(s, w_ref[...], preferred_element_type=F32,
                         precision=lax.Precision.HIGHEST) + b_ref[...]


def _adaln(cond, w, b):
    n = w.shape[1]
    tn = n // 4
    return pl.pallas_call(
        _adaln_kernel,
        grid=(n // tn,),
        in_specs=[_const_spec((COND_ROWS, D_MODEL)),
                  pl.BlockSpec((D_MODEL, tn), lambda j: (0, j)),
                  pl.BlockSpec((1, tn), lambda j: (0, j))],
        out_specs=pl.BlockSpec((COND_ROWS, tn), lambda j: (0, j)),
        out_shape=jax.ShapeDtypeStruct((COND_ROWS, n), F32),
        compiler_params=_params("parallel"),
        name="adaln",
    )(cond, w, b.reshape(1, n))


def _in_proj_kernel(x_ref, shift_ref, scale_ref, gnorm_ref, wqkvT_ref, wf_ref, gq_ref, gk_ref,
                    cos_ref, sin_ref, qT_ref, k_ref, vT_ref, f_ref):
    tm = x_ref.shape[1]
    gq, gk = gq_ref[...], gk_ref[...]
    half = HEAD_DIM // 2
    for j in range(tm // TM_SUB):
        tok = slice(j * TM_SUB, (j + 1) * TM_SUB)
        h = _modulated_norm(x_ref[0, tok, :], gnorm_ref[...], shift_ref[0], scale_ref[0]).astype(BF16)
        zT = lax.dot_general(wqkvT_ref[...], h, (((1,), (1,)), ((), ())), preferred_element_type=F32)
        cos, sin = cos_ref[:, tok], sin_ref[:, tok]
        for i in range(N_HEADS):
            lo = i * HEAD_DIM
            a, b = _rope_t(_head_norm_t(zT[lo:lo + HEAD_DIM], gq), cos, sin)
            qT_ref[0, lo:lo + half, tok] = a.astype(BF16)
            qT_ref[0, lo + half:lo + HEAD_DIM, tok] = b.astype(BF16)
        k_parts = []
        for i in range(N_KV_HEADS):
            lo = Q_W + i * HEAD_DIM
            k_parts.extend(_rope_t(_head_norm_t(zT[lo:lo + HEAD_DIM], gk), cos, sin))
        k_tok = jnp.concatenate(k_parts, axis=0).T
        for g in range(N_KV_HEADS):
            k_ref[0, g, tok, :] = k_tok[:, g * HEAD_DIM:(g + 1) * HEAD_DIM].astype(BF16)
        vT_ref[0, :, tok] = zT[K_END:V_END].astype(BF16)
        f_ref[0, tok, :] = jnp.dot(h, wf_ref[...], preferred_element_type=F32).astype(BF16)


def _in_proj(x, mods3, gnorm, wqkvT, wf, gq_b, gk_b, cosT, sinT, n_ctx):
    B, S, D = x.shape
    tm = TM_PROJ
    half = HEAD_DIM // 2
    return pl.pallas_call(
        _in_proj_kernel,
        grid=(B, S // tm),
        in_specs=[pl.BlockSpec((1, tm, D), lambda b, i: (b, i, 0)),
                  _mod_spec(0), _mod_spec(1),
                  _const_spec((1, D)),
                  _const_spec((V_END, D)), _const_spec((D, F_W)),
                  _const_spec((HEAD_DIM, TM_SUB)), _const_spec((HEAD_DIM, TM_SUB)),
                  pl.BlockSpec((half, tm), lambda b, i: (0, i)),
                  pl.BlockSpec((half, tm), lambda b, i: (0, i))],
        out_specs=[pl.BlockSpec((1, Q_W, tm), lambda b, i: (b, 0, i)),
                   pl.BlockSpec((1, N_KV_HEADS, tm, HEAD_DIM), lambda b, i: (b, 0, i, 0)),
                   pl.BlockSpec((1, KV_W, tm), lambda b, i: (b, 0, i)),
                   pl.BlockSpec((1, tm, F_W), lambda b, i: (b, i, 0))],
        out_shape=[jax.ShapeDtypeStruct((B, Q_W, S), BF16),
                   jax.ShapeDtypeStruct((B, N_KV_HEADS, S + n_ctx, HEAD_DIM), BF16),
                   jax.ShapeDtypeStruct((B, KV_W, S + n_ctx), BF16),
                   jax.ShapeDtypeStruct((B, S, F_W), BF16)],
        compiler_params=_params("parallel", "parallel"),
        name="in_proj",
    )(x, mods3, mods3, gnorm, wqkvT, wf, gq_b, gk_b, cosT, sinT)


def _ctx_kv_kernel(x_ref, shift_ref, scale_ref, gnorm_ref, wkvT_ref, gk_ref, k_in_ref, vT_in_ref,
                   k_ref, vT_ref):
    del k_in_ref, vT_in_ref
    h = _modulated_norm(x_ref[0], gnorm_ref[...], shift_ref[0], scale_ref[0]).astype(BF16)
    zT = lax.dot_general(wkvT_ref[...], h, (((1,), (1,)), ((), ())), preferred_element_type=F32)
    gk = gk_ref[...]
    kT = jnp.concatenate(
        [_head_norm_t(zT[i * HEAD_DIM:(i + 1) * HEAD_DIM], gk) for i in range(N_KV_HEADS)], axis=0)
    k_tok = kT.T
    for g in range(N_KV_HEADS):
        k_ref[0, g] = k_tok[:, g * HEAD_DIM:(g + 1) * HEAD_DIM].astype(BF16)
    vT_ref[0] = zT[KV_W:].astype(BF16)


def _ctx_kv(ctx, mods3, gnorm, wkvT, gk_b, k_all, vT_all):
    B, C, D = ctx.shape
    n_lat = k_all.shape[2] - C
    assert n_lat % C == 0
    blk = n_lat // C
    return pl.pallas_call(
        _ctx_kv_kernel,
        grid=(B,),
        in_specs=[pl.BlockSpec((1, C, D), lambda b: (b, 0, 0)),
                  _mod_spec(0, row=B), _mod_spec(1, row=B), _const_spec((1, D)),
                  _const_spec((2 * KV_W, D)), _const_spec((HEAD_DIM, C)),
                  pl.BlockSpec(memory_space=pl.ANY), pl.BlockSpec(memory_space=pl.ANY)],
        out_specs=[pl.BlockSpec((1, N_KV_HEADS, C, HEAD_DIM), lambda b: (b, 0, blk, 0)),
                   pl.BlockSpec((1, KV_W, C), lambda b: (b, 0, blk))],
        out_shape=[jax.ShapeDtypeStruct(k_all.shape, BF16),
                   jax.ShapeDtypeStruct(vT_all.shape, BF16)],
        input_output_aliases={6: 0, 7: 1},
        compiler_params=_params("parallel"),
        name="ctx_kv",
    )(ctx, mods3, mods3, gnorm, wkvT, gk_b, k_all, vT_all)


def _attention_kernel(qT_ref, k_ref, vT_ref, o_ref, vaug_ref, s0_ref, s1_ref, m0_ref, m1_ref, oT_ref):
    qi = pl.program_id(1)
    tq = qT_ref.shape[2]
    n_chunks = k_ref.shape[2] // KEY_CHUNK

    @pl.when(qi == 0)
    def _():
        ones = jnp.ones((ONES_ROWS, KEY_CHUNK), BF16)
        for g in range(N_KV_HEADS):
            for c in range(n_chunks):
                vaug_ref[g, c, :HEAD_DIM, :] = vT_ref[0, g * HEAD_DIM:(g + 1) * HEAD_DIM,
                                                      c * KEY_CHUNK:(c + 1) * KEY_CHUNK]
                vaug_ref[g, c, HEAD_DIM:, :] = ones

    def head_rows(t):
        return slice(t * HEAD_DIM, (t + 1) * HEAD_DIM)

    def step(t_b, sb, sa):
        m8, oa = jnp.full((8, tq), -jnp.inf, F32), None
        for c in range(n_chunks):
            rows = slice(c * KEY_CHUNK, (c + 1) * KEY_CHUNK)
            srows = pl.ds(pl.multiple_of(row0 + c * KEY_CHUNK, KEY_CHUNK), KEY_CHUNK)
            if sb is not None:
                p = jnp.exp2(sb[0][srows, :] - sb[1][0:1, :]).astype(BF16)
                d = jnp.dot(vaug_ref[t_b // GROUP, c], p, preferred_element_type=F32)
                oa = d if oa is None else oa + d
            if sa is not None:
                s = jnp.dot(k_ref[0, (t_b + 1) // GROUP, rows, :], qT_ref[0, head_rows(t_b + 1), :],
                            preferred_element_type=F32)
                sa[0][srows, :] = s
                m8 = jnp.maximum(m8, jnp.max(s.reshape(KEY_CHUNK // 8, 8, tq), axis=0))
        if sa is not None:
            sa[1][...] = jnp.broadcast_to(jnp.max(m8, axis=0, keepdims=True), (8, tq))
        if sb is not None:
            oT_ref[head_rows(t_b), :] = oa[:HEAD_DIM] / oa[HEAD_DIM:HEAD_DIM + 1]

    buf0, buf1 = (s0_ref, m0_ref), (s1_ref, m1_ref)
    row0 = jnp.minimum(pl.program_id(0), 0)
    step(-1, None, buf0)

    bufs = (buf0, buf1)
    for t in range(N_HEADS - 1):
        step(t, bufs[t % 2], bufs[(t + 1) % 2])
    step(N_HEADS - 1, bufs[(N_HEADS - 1) % 2], None)
    o_ref[0] = oT_ref[...].T.astype(BF16)


def _attention(qT, k_all, vT_all):
    B, _, S = qT.shape
    T = k_all.shape[2]
    assert T % KEY_CHUNK == 0
    n_chunks = T // KEY_CHUNK
    return pl.pallas_call(
        _attention_kernel,
        grid=(B, S // TQ),
        in_specs=[pl.BlockSpec((1, Q_W, TQ), lambda b, i: (b, 0, i)),
                  pl.BlockSpec((1, N_KV_HEADS, T, HEAD_DIM), lambda b, i: (b, 0, 0, 0)),
                  pl.BlockSpec((1, KV_W, T), lambda b, i: (b, 0, 0))],
        out_specs=pl.BlockSpec((1, TQ, Q_W), lambda b, i: (b, i, 0)),
        out_shape=jax.ShapeDtypeStruct((B, S, Q_W), BF16),
        scratch_shapes=[pltpu.VMEM((N_KV_HEADS, n_chunks, HEAD_DIM + ONES_ROWS, KEY_CHUNK), BF16),
                        pltpu.VMEM((T, TQ), F32), pltpu.VMEM((T, TQ), F32),
                        pltpu.VMEM((8, TQ), F32), pltpu.VMEM((8, TQ), F32),
                        pltpu.VMEM((Q_W, TQ), F32)],
        compiler_params=_params("parallel", "arbitrary"),
        name="attention",
    )(qT, k_all, vT_all)


@functools.lru_cache(maxsize=None)
def _dft_tables(n_pos):
    def cos_sin(n):
        idx = np.arange(n, dtype=np.int64)
        ang = 2.0 * np.pi * ((idx[:, None] * idx[None, :]) % n).astype(np.float64) / n
        return np.cos(ang), np.sin(ang)
    cc, sc = cos_sin(FGROUP_DIM)
    cl, sl = cos_sin(n_pos)
    chan = np.concatenate([cc, sc], axis=1).astype(np.float32)
    pos = np.concatenate([cl, -sl], axis=1).astype(np.float32)
    return chan, pos


def _fourier_kernel(f_ref, chan_ref, pos_ref, o_ref, z_ref, *, scale):
    n_pos = f_ref.shape[1]
    for g in range(N_FGROUPS):
        cols = slice(g * FGROUP_DIM, (g + 1) * FGROUP_DIM)
        xcs = jnp.dot(f_ref[0, :, cols], chan_ref[...], preferred_element_type=F32)
        z_ref[0:n_pos, cols] = xcs[:, :FGROUP_DIM].astype(BF16)
        z_ref[n_pos:2 * n_pos, cols] = xcs[:, FGROUP_DIM:].astype(BF16)
    y = jnp.dot(pos_ref[...], z_ref[...], preferred_element_type=F32)
    o_ref[0] = (y * scale).astype(BF16)


def _fourier(f):
    B, S, W = f.shape
    chan, pos = _dft_tables(S)
    scale = float(1.0 / np.sqrt(S * FGROUP_DIM))
    return pl.pallas_call(
        functools.partial(_fourier_kernel, scale=scale),
        grid=(B,),
        in_specs=[pl.BlockSpec((1, S, W), lambda b: (b, 0, 0)),
                  _const_spec((FGROUP_DIM, 2 * FGROUP_DIM)),
                  _const_spec((S, 2 * S))],
        out_specs=pl.BlockSpec((1, S, W), lambda b: (b, 0, 0)),
        out_shape=jax.ShapeDtypeStruct((B, S, W), BF16),
        scratch_shapes=[pltpu.VMEM((2 * S, W), BF16)],
        compiler_params=_params("parallel"),
        name="fourier",
    )(f, jnp.asarray(chan).astype(BF16), jnp.asarray(pos).astype(BF16))


def _merge_kernel(x_ref, shift_ref, scale_ref, gate_ref, gnorm_ref, wg_ref, bg_ref, attn_ref, four_ref,
                  wap_ref, wfp_ref, wo_ref, o_ref):
    x = x_ref[0]
    h = _modulated_norm(x, gnorm_ref[...], shift_ref[0], scale_ref[0]).astype(BF16)
    gates = _sigmoid(jnp.dot(h, wg_ref[...], preferred_element_type=F32) + bg_ref[...])
    a = jnp.dot(attn_ref[0], wap_ref[...], preferred_element_type=F32)
    fo = jnp.dot(four_ref[0], wfp_ref[...], preferred_element_type=F32)
    merged = gates[:, :D_MODEL] * a + gates[:, D_MODEL:] * fo
    mix = jnp.dot(merged.astype(BF16), wo_ref[...], preferred_element_type=F32)
    o_ref[0] = x + gate_ref[0] * mix


def _merge(x, mods3, gnorm, wg, bg, attn, four, wap, wfp, wo):
    B, S, D = x.shape
    tm = TM_MERGE
    return pl.pallas_call(
        _merge_kernel,
        grid=(B, S // tm),
        in_specs=[pl.BlockSpec((1, tm, D), lambda b, i: (b, i, 0)), _mod_spec(0), _mod_spec(1), _mod_spec(2),
                  _const_spec((1, D)),
                  _const_spec((D, N_BRANCH * D)), _const_spec((1, N_BRANCH * D)),
                  pl.BlockSpec((1, tm, Q_W), lambda b, i: (b, i, 0)),
                  pl.BlockSpec((1, tm, F_W), lambda b, i: (b, i, 0)),
                  _const_spec((Q_W, D)), _const_spec((F_W, D)), _const_spec((D, D))],
        out_specs=pl.BlockSpec((1, tm, D), lambda b, i: (b, i, 0)),
        out_shape=jax.ShapeDtypeStruct((B, S, D), F32),
        compiler_params=_params("parallel", "parallel"),
        name="merge",
    )(x, mods3, mods3, mods3, gnorm, wg, bg, attn, four, wap, wfp, wo)


def _ffn_kernel(x_ref, shift_ref, scale_ref, gate_ref, gnorm_ref, wgu_ref, wdown_ref, gfinal_ref, o_ref):
    x = x_ref[0]
    d_ff = wdown_ref.shape[0]
    h = _modulated_norm(x, gnorm_ref[...], shift_ref[0], scale_ref[0]).astype(BF16)
    ffn = None
    for lo in range(0, d_ff, FF_CHUNK):
        gate = jnp.dot(h, wgu_ref[:, lo:lo + FF_CHUNK], preferred_element_type=F32)
        up = jnp.dot(h, wgu_ref[:, d_ff + lo:d_ff + lo + FF_CHUNK], preferred_element_type=F32)
        act = (gate * _sigmoid(gate) * up).astype(BF16)
        part = jnp.dot(act, wdown_ref[lo:lo + FF_CHUNK, :], preferred_element_type=F32)
        ffn = part if ffn is None else ffn + part
    x2 = x + gate_ref[0] * ffn
    ms = jnp.mean(x2 * x2, axis=-1, keepdims=True)
    o_ref[0] = x2 * lax.rsqrt(ms + EPS) * gfinal_ref[...]


def _ffn(x, mods3, gnorm, wgu, wdown, gfinal):
    B, S, D = x.shape
    d_ff = wdown.shape[0]
    tm = TM_FFN
    return pl.pallas_call(
        _ffn_kernel,
        grid=(B, S // tm),
        in_specs=[pl.BlockSpec((1, tm, D), lambda b, i: (b, i, 0)), _mod_spec(3), _mod_spec(4), _mod_spec(5),
                  _const_spec((1, D)),
                  _const_spec((D, 2 * d_ff)), _const_spec((d_ff, D)), _const_spec((1, D))],
        out_specs=pl.BlockSpec((1, tm, D), lambda b, i: (b, i, 0)),
        out_shape=jax.ShapeDtypeStruct((B, S, D), F32),
        compiler_params=_params("parallel", "parallel"),
        name="ffn",
    )(x, mods3, mods3, mods3, gnorm, wgu, wdown, gfinal)


@functools.lru_cache(maxsize=None)
def _rope_tables_t(seq):
    rows = seq // GRID_W
    t_row = np.repeat(np.arange(rows, dtype=np.float64), GRID_W)
    t_col = np.tile(np.arange(GRID_W, dtype=np.float64), rows)
    inv_freq = ROPE_THETA ** (-np.arange(0, ROPE_AXIS_DIM, 2, dtype=np.float64) / ROPE_AXIS_DIM)
    ang = np.concatenate([t_row[:, None] * inv_freq, t_col[:, None] * inv_freq], axis=-1)
    return np.cos(ang).T.astype(np.float32), np.sin(ang).T.astype(np.float32)


def kernel(x, c, ctx, c_ctx, w_ada, b_ada, g_norm_mix, g_norm_ffn, w_in, b_gate, g_q, g_k, w_attn_proj,
           w_fourier_proj, w_o, w_gate_up, w_down, g_final):
    B, S, D = x.shape
    C = ctx.shape[1]
    assert D == D_MODEL and w_in.shape[0] == 1 and B < COND_ROWS
    assert S % TM_PROJ == 0 and TM_PROJ % TM_SUB == 0 and S % TM_MERGE == 0 and S % TM_FFN == 0 and S % TQ == 0
    assert w_down.shape[1] % FF_CHUNK == 0
    layer = 0

    cond = jnp.zeros((COND_ROWS, D), F32).at[:B].set(c).at[B].set(c_ctx)
    mods = _adaln(cond, w_ada[layer], b_ada[layer])
    mods3 = mods.reshape(COND_ROWS * N_MODS, 1, D)

    w_in_l = w_in[layer]
    wqkvT = w_in_l[:, :V_END].T.astype(BF16)
    wkvT = w_in_l[:, Q_W:V_END].T.astype(BF16)
    wf = w_in_l[:, V_END:F_END].astype(BF16)
    wg = w_in_l[:, F_END:].astype(BF16)
    gnorm_mix = g_norm_mix[layer].reshape(1, D)
    gnorm_ffn = g_norm_ffn[layer].reshape(1, D)

    gq_col = (g_q[layer] * (HEAD_DIM ** -0.5 * LOG2_E)).reshape(HEAD_DIM, 1)
    gk_col = g_k[layer].reshape(HEAD_DIM, 1)
    gq_b = jnp.broadcast_to(gq_col, (HEAD_DIM, TM_SUB))
    gk_b = jnp.broadcast_to(gk_col, (HEAD_DIM, TM_SUB))
    gk_ctx = jnp.broadcast_to(gk_col, (HEAD_DIM, C))
    cosT, sinT = (jnp.asarray(t) for t in _rope_tables_t(S))

    qT, k_all, vT_all, f = _in_proj(x, mods3, gnorm_mix, wqkvT, wf, gq_b, gk_b, cosT, sinT, C)
    k_all, vT_all = _ctx_kv(ctx, mods3, gnorm_mix, wkvT, gk_ctx, k_all, vT_all)
    attn = _attention(qT, k_all, vT_all)
    four = _fourier(f)
    x1 = _merge(x, mods3, gnorm_mix, wg, b_gate[layer].reshape(1, -1), attn, four,
                w_attn_proj[layer].astype(BF16), w_fourier_proj[layer].astype(BF16), w_o[layer].astype(BF16))
    return _ffn(x1, mods3, gnorm_ffn, w_gate_up[layer].astype(BF16), w_down[layer].astype(BF16),
                g_final.reshape(1, D))
```

```python
import functools

import numpy as np
import jax
import jax.numpy as jnp
from jax import lax
from jax.experimental import pallas as pl
from jax.experimental.pallas import tpu as pltpu

D_MODEL = 1024
GRID_W = 64
HEAD_DIM = 64
N_HEADS = 16
N_KV_HEADS = 4
GROUP = N_HEADS // N_KV_HEADS
ROPE_AXIS_DIM = HEAD_DIM // 2
ROPE_THETA = 10000.0
N_FGROUPS = 4
FGROUP_DIM = 128
N_BRANCH = 2
Q_W = N_HEADS * HEAD_DIM
KV_W = N_KV_HEADS * HEAD_DIM
F_W = N_FGROUPS * FGROUP_DIM
K_END = Q_W + KV_W
V_END = K_END + KV_W
F_END = V_END + F_W
EPS = 1e-6
LOG2_E = 1.4426950408889634

BF16 = jnp.bfloat16
F32 = jnp.float32

VMEM_LIMIT_BYTES = 56 * 1024 * 1024

COND_ROWS = 16
N_MODS = 6
ONES_ROWS = 16
TM_PROJ = 2048
TM_SUB = 512
TM_MERGE = 1024
TM_FFN = 1024
FF_CHUNK = 256
TQ = 512
KEY_CHUNK = 256


def _params(*sem):
    return pltpu.CompilerParams(dimension_semantics=sem, vmem_limit_bytes=VMEM_LIMIT_BYTES)


def _const_spec(shape):
    nd = len(shape)
    return pl.BlockSpec(shape, lambda *_: (0,) * nd, pipeline_mode=pl.Buffered(1))


def _mod_spec(k, row=None):
    if row is None:
        return pl.BlockSpec((1, 1, D_MODEL), lambda b, *_: (b * N_MODS + k, 0, 0))
    return pl.BlockSpec((1, 1, D_MODEL), lambda *_: (row * N_MODS + k, 0, 0))


def _sigmoid(x):
    return 1.0 / (1.0 + jnp.exp(-x))


def _modulated_norm(x, gain, shift, scale):
    ms = jnp.mean(x * x, axis=-1, keepdims=True)
    y = x * lax.rsqrt(ms + EPS) * gain
    return y * (1.0 + scale) + shift


def _head_norm_t(zh, gain):
    ms = jnp.mean(zh * zh, axis=0, keepdims=True)
    return zh * lax.rsqrt(ms + EPS) * gain


def _rope_t(zn, cos, sin):
    half = HEAD_DIM // 2
    x1, x2 = zn[:half], zn[half:]
    return x1 * cos - x2 * sin, x2 * cos + x1 * sin


def _adaln_kernel(cond_ref, w_ref, b_ref, o_ref):
    c = cond_ref[...]
    s = c * _sigmoid(c)
    o_ref[...] = jnp.dot(s, w_ref[...], preferred_element_type=F32,
                         precision=lax.Precision.HIGHEST) + b_ref[...]


def _adaln(cond, w, b):
    n = w.shape[1]
    tn = n // 4
    return pl.pallas_call(
        _adaln_kernel,
        grid=(n // tn,),
        in_specs=[_const_spec((COND_ROWS, D_MODEL)),
                  pl.BlockSpec((D_MODEL, tn), lambda j: (0, j)),
                  pl.BlockSpec((1, tn), lambda j: (0, j))],
        out_specs=pl.BlockSpec((COND_ROWS, tn), lambda j: (0, j)),
        out_shape=jax.ShapeDtypeStruct((COND_ROWS, n), F32),
        compiler_params=_params("parallel"),
        name="adaln",
    )(cond, w, b.reshape(1, n))


def _in_proj_kernel(x_ref, shift_ref, scale_ref, gnorm_ref, wqkvT_ref, wf_ref, gq_ref, gk_ref,
                    cos_ref, sin_ref, qT_ref, k_ref, vT_ref, f_ref):
    tm = x_ref.shape[1]
    gq, gk = gq_ref[...], gk_ref[...]
    half = HEAD_DIM // 2
    for j in range(tm // TM_SUB):
        tok = slice(j * TM_SUB, (j + 1) * TM_SUB)
        h = _modulated_norm(x_ref[0, tok, :], gnorm_ref[...], shift_ref[0], scale_ref[0]).astype(BF16)
        zT = lax.dot_general(wqkvT_ref[...], h, (((1,), (1,)), ((), ())), preferred_element_type=F32)
        cos, sin = cos_ref[:, tok], sin_ref[:, tok]
        for i in range(N_HEADS):
            lo = i * HEAD_DIM
            a, b = _rope_t(_head_norm_t(zT[lo:lo + HEAD_DIM], gq), cos, sin)
            qT_ref[0, lo:lo + half, tok] = a.astype(BF16)
            qT_ref[0, lo + half:lo + HEAD_DIM, tok] = b.astype(BF16)
        k_parts = []
        for i in range(N_KV_HEADS):
            lo = Q_W + i * HEAD_DIM
            k_parts.extend(_rope_t(_head_norm_t(zT[lo:lo + HEAD_DIM], gk), cos, sin))
        k_tok = jnp.concatenate(k_parts, axis=0).T
        for g in range(N_KV_HEADS):
            k_ref[0, g, tok, :] = k_tok[:, g * HEAD_DIM:(g + 1) * HEAD_DIM].astype(BF16)
        vT_ref[0, :, tok] = zT[K_END:V_END].astype(BF16)
        f_ref[0, tok, :] = jnp.dot(h, wf_ref[...], preferred_element_type=F32).astype(BF16)


def _in_proj(x, mods3, gnorm, wqkvT, wf, gq_b, gk_b, cosT, sinT, n_ctx):
    B, S, D = x.shape
    tm = TM_PROJ
    half = HEAD_DIM // 2
    return pl.pallas_call(
        _in_proj_kernel,
        grid=(B, S // tm),
        in_specs=[pl.BlockSpec((1, tm, D), lambda b, i: (b, i, 0)),
                  _mod_spec(0), _mod_spec(1),
                  _const_spec((1, D)),
                  _const_spec((V_END, D)), _const_spec((D, F_W)),
                  _const_spec((HEAD_DIM, TM_SUB)), _const_spec((HEAD_DIM, TM_SUB)),
                  pl.BlockSpec((half, tm), lambda b, i: (0, i)),
                  pl.BlockSpec((half, tm), lambda b, i: (0, i))],
        out_specs=[pl.BlockSpec((1, Q_W, tm), lambda b, i: (b, 0, i)),
                   pl.BlockSpec((1, N_KV_HEADS, tm, HEAD_DIM), lambda b, i: (b, 0, i, 0)),
                   pl.BlockSpec((1, KV_W, tm), lambda b, i: (b, 0, i)),
                   pl.BlockSpec((1, tm, F_W), lambda b, i: (b, i, 0))],
        out_shape=[jax.ShapeDtypeStruct((B, Q_W, S), BF16),
                   jax.ShapeDtypeStruct((B, N_KV_HEADS, S + n_ctx, HEAD_DIM), BF16),
                   jax.ShapeDtypeStruct((B, KV_W, S + n_ctx), BF16),
                   jax.ShapeDtypeStruct((B, S, F_W), BF16)],
        compiler_params=_params("parallel", "parallel"),
        name="in_proj",
    )(x, mods3, mods3, gnorm, wqkvT, wf, gq_b, gk_b, cosT, sinT)


def _ctx_kv_kernel(x_ref, shift_ref, scale_ref, gnorm_ref, wkvT_ref, gk_ref, k_in_ref, vT_in_ref,
                   k_ref, vT_ref):
    del k_in_ref, vT_in_ref
    h = _modulated_norm(x_ref[0], gnorm_ref[...], shift_ref[0], scale_ref[0]).astype(BF16)
    zT = lax.dot_general(wkvT_ref[...], h, (((1,), (1,)), ((), ())), preferred_element_type=F32)
    gk = gk_ref[...]
    kT = jnp.concatenate(
        [_head_norm_t(zT[i * HEAD_DIM:(i + 1) * HEAD_DIM], gk) for i in range(N_KV_HEADS)], axis=0)
    k_tok = kT.T
    for g in range(N_KV_HEADS):
        k_ref[0, g] = k_tok[:, g * HEAD_DIM:(g + 1) * HEAD_DIM].astype(BF16)
    vT_ref[0] = zT[KV_W:].astype(BF16)


def _ctx_kv(ctx, mods3, gnorm, wkvT, gk_b, k_all, vT_all):
    B, C, D = ctx.shape
    n_lat = k_all.shape[2] - C
    assert n_lat % C == 0
    blk = n_lat // C
    return pl.pallas_call(
        _ctx_kv_kernel,
        grid=(B,),
        in_specs=[pl.BlockSpec((1, C, D), lambda b: (b, 0, 0)),
                  _mod_spec(0, row=B), _mod_spec(1, row=B), _const_spec((1, D)),
                  _const_spec((2 * KV_W, D)), _const_spec((HEAD_DIM, C)),
                  pl.BlockSpec(memory_space=pl.ANY), pl.BlockSpec(memory_space=pl.ANY)],
        out_specs=[pl.BlockSpec((1, N_KV_HEADS, C, HEAD_DIM), lambda b: (b, 0, blk, 0)),
                   pl.BlockSpec((1, KV_W, C), lambda b: (b, 0, blk))],
        out_shape=[jax.ShapeDtypeStruct(k_all.shape, BF16),
                   jax.ShapeDtypeStruct(vT_all.shape, BF16)],
        input_output_aliases={6: 0, 7: 1},
        compiler_params=_params("parallel"),
        name="ctx_kv",
    )(ctx, mods3, mods3, gnorm, wkvT, gk_b, k_all, vT_all)


def _attention_kernel(qT_ref, k_ref, vT_ref, o_ref, vaug_ref, s0_ref, s1_ref, m0_ref, m1_ref, oT_ref):
    qi = pl.program_id(1)
    tq = qT_ref.shape[2]
    n_chunks = k_ref.shape[2] // KEY_CHUNK

    @pl.when(qi == 0)
    def _():
        ones = jnp.ones((ONES_ROWS, KEY_CHUNK), BF16)
        for g in range(N_KV_HEADS):
            for c in range(n_chunks):
                vaug_ref[g, c, :HEAD_DIM, :] = vT_ref[0, g * HEAD_DIM:(g + 1) * HEAD_DIM,
                                                      c * KEY_CHUNK:(c + 1) * KEY_CHUNK]
                vaug_ref[g, c, HEAD_DIM:, :] = ones

    def head_rows(t):
        return slice(t * HEAD_DIM, (t + 1) * HEAD_DIM)

    def step(t_b, sb, sa):
        m8, oa = jnp.full((8, tq), -jnp.inf, F32), None
        for c in range(n_chunks):
            rows = slice(c * KEY_CHUNK, (c + 1) * KEY_CHUNK)
            srows = pl.ds(pl.multiple_of(row0 + c * KEY_CHUNK, KEY_CHUNK), KEY_CHUNK)
            if sb is not None:
                p = jnp.exp2(sb[0][srows, :] - sb[1][0:1, :]).astype(BF16)
                d = jnp.dot(vaug_ref[t_b // GROUP, c], p, preferred_element_type=F32)
                oa = d if oa is None else oa + d
            if sa is not None:
                s = jnp.dot(k_ref[0, (t_b + 1) // GROUP, rows, :], qT_ref[0, head_rows(t_b + 1), :],
                            preferred_element_type=F32)
                sa[0][srows, :] = s
                m8 = jnp.maximum(m8, jnp.max(s.reshape(KEY_CHUNK // 8, 8, tq), axis=0))
        if sa is not None:
            sa[1][...] = jnp.broadcast_to(jnp.max(m8, axis=0, keepdims=True), (8, tq))
        if sb is not None:
            oT_ref[head_rows(t_b), :] = oa[:HEAD_DIM] / oa[HEAD_DIM:HEAD_DIM + 1]

    buf0, buf1 = (s0_ref, m0_ref), (s1_ref, m1_ref)
    row0 = jnp.minimum(pl.program_id(0), 0)
    step(-1, None, buf0)

    bufs = (buf0, buf1)
    for t in range(N_HEADS - 1):
        step(t, bufs[t % 2], bufs[(t + 1) % 2])
    step(N_HEADS - 1, bufs[(N_HEADS - 1) % 2], None)
    o_ref[0] = oT_ref[...].T.astype(BF16)


def _attention(qT, k_all, vT_all):
    B, _, S = qT.shape
    T = k_all.shape[2]
    assert T % KEY_CHUNK == 0
    n_chunks = T // KEY_CHUNK
    return pl.pallas_call(
        _attention_kernel,
        grid=(B, S // TQ),
        in_specs=[pl.BlockSpec((1, Q_W, TQ), lambda b, i: (b, 0, i)),
                  pl.BlockSpec((1, N_KV_HEADS, T, HEAD_DIM), lambda b, i: (b, 0, 0, 0)),
                  pl.BlockSpec((1, KV_W, T), lambda b, i: (b, 0, 0))],
        out_specs=pl.BlockSpec((1, TQ, Q_W), lambda b, i: (b, i, 0)),
        out_shape=jax.ShapeDtypeStruct((B, S, Q_W), BF16),
        scratch_shapes=[pltpu.VMEM((N_KV_HEADS, n_chunks, HEAD_DIM + ONES_ROWS, KEY_CHUNK), BF16),
                        pltpu.VMEM((T, TQ), F32), pltpu.VMEM((T, TQ), F32),
                        pltpu.VMEM((8, TQ), F32), pltpu.VMEM((8, TQ), F32),
                        pltpu.VMEM((Q_W, TQ), F32)],
        compiler_params=_params("parallel", "arbitrary"),
        name="attention",
    )(qT, k_all, vT_all)


POS_RADIX = 4


@functools.lru_cache(maxsize=None)
def _dft_tables(n_pos):
    def cos_sin(rows, cols, period):
        ang = 2.0 * np.pi * ((rows[:, None] * cols[None, :]) % period).astype(np.float64) / period
        return np.cos(ang), np.sin(ang)
    q = n_pos // POS_RADIX
    ch = np.arange(FGROUP_DIM, dtype=np.int64)
    cc, sc = cos_sin(ch, ch, FGROUP_DIM)
    n2 = np.arange(q, dtype=np.int64)
    cq, sq = cos_sin(n2, n2, q)
    tc, ts = cos_sin(np.arange(POS_RADIX, dtype=np.int64), n2, n_pos)
    lanes = np.ones((1, F_W))
    chan = np.concatenate([cc, sc], axis=1).astype(np.float32)
    pos = np.concatenate([cq, sq], axis=1).astype(np.float32)
    twc = (tc.reshape(n_pos, 1) * lanes).astype(np.float32)
    tws = (ts.reshape(n_pos, 1) * lanes).astype(np.float32)
    return chan, pos, twc, tws


def _fourier_kernel(f_ref, chan_ref, pos_ref, twc_ref, tws_ref, o_ref, ur_ref, ui_ref, t_ref, y_ref, *, scale):
    assert POS_RADIX == 4
    n_pos = f_ref.shape[1]
    q = n_pos // POS_RADIX
    for g in range(N_FGROUPS):
        cols = slice(g * FGROUP_DIM, (g + 1) * FGROUP_DIM)
        xcs = jnp.dot(f_ref[0, :, cols], chan_ref[...], preferred_element_type=F32)
        ur_ref[:, cols] = xcs[:, :FGROUP_DIM]
        ui_ref[:, cols] = -xcs[:, FGROUP_DIM:]
    ur = [ur_ref[i * q:(i + 1) * q, :] for i in range(POS_RADIX)]
    ui = [ui_ref[i * q:(i + 1) * q, :] for i in range(POS_RADIX)]
    er, ei = ur[0] + ur[2], ui[0] + ui[2]
    sr, si = ur[1] + ur[3], ui[1] + ui[3]
    dr, di = ur[0] - ur[2], ui[0] - ui[2]
    gr, gi = ur[1] - ur[3], ui[1] - ui[3]
    butterflies = [(er + sr, ei + si), (dr + gi, di - gr), (er - sr, ei - si), (dr - gi, di + gr)]
    for k1, (tr, ti) in enumerate(butterflies):
        if k1:
            c = twc_ref[k1 * q:(k1 + 1) * q, :]
            s = tws_ref[k1 * q:(k1 + 1) * q, :]
            tr, ti = tr * c + ti * s, ti * c - tr * s
        t_ref[k1, 0:q, :] = tr.astype(BF16)
        t_ref[k1, q:2 * q, :] = ti.astype(BF16)
    for k1 in range(POS_RADIX):
        y = jnp.dot(pos_ref[...], t_ref[k1], preferred_element_type=F32) * scale
        for g in range(N_FGROUPS):
            y_ref[g, pl.ds(k1, q, stride=POS_RADIX), :] = y[:, g * FGROUP_DIM:(g + 1) * FGROUP_DIM]
    o_ref[0] = jnp.concatenate([y_ref[g] for g in range(N_FGROUPS)], axis=1).astype(BF16)


def _fourier(f):
    B, S, W = f.shape
    assert S % POS_RADIX == 0
    q = S // POS_RADIX
    chan, pos, twc, tws = _dft_tables(S)
    scale = float(1.0 / np.sqrt(S * FGROUP_DIM))
    return pl.pallas_call(
        functools.partial(_fourier_kernel, scale=scale),
        grid=(B,),
        in_specs=[pl.BlockSpec((1, S, W), lambda b: (b, 0, 0)),
                  _const_spec((FGROUP_DIM, 2 * FGROUP_DIM)),
                  _const_spec((q, 2 * q)),
                  _const_spec((S, W)), _const_spec((S, W))],
        out_specs=pl.BlockSpec((1, S, W), lambda b: (b, 0, 0)),
        out_shape=jax.ShapeDtypeStruct((B, S, W), BF16),
        scratch_shapes=[pltpu.VMEM((S, W), F32), pltpu.VMEM((S, W), F32),
                        pltpu.VMEM((POS_RADIX, 2 * q, W), BF16),
                        pltpu.VMEM((N_FGROUPS, S, FGROUP_DIM), F32)],
        compiler_params=_params("parallel"),
        name="fourier",
    )(f, jnp.asarray(chan).astype(BF16), jnp.asarray(pos).astype(BF16), jnp.asarray(twc), jnp.asarray(tws))


def _merge_kernel(x_ref, shift_ref, scale_ref, gate_ref, gnorm_ref, wg_ref, bg_ref, attn_ref, four_ref,
                  wap_ref, wfp_ref, wo_ref, o_ref):
    x = x_ref[0]
    h = _modulated_norm(x, gnorm_ref[...], shift_ref[0], scale_ref[0]).astype(BF16)
    gates = _sigmoid(jnp.dot(h, wg_ref[...], preferred_element_type=F32) + bg_ref[...])
    a = jnp.dot(attn_ref[0], wap_ref[...], preferred_element_type=F32)
    fo = jnp.dot(four_ref[0], wfp_ref[...], preferred_element_type=F32)
    merged = gates[:, :D_MODEL] * a + gates[:, D_MODEL:] * fo
    mix = jnp.dot(merged.astype(BF16), wo_ref[...], preferred_element_type=F32)
    o_ref[0] = x + gate_ref[0] * mix


def _merge(x, mods3, gnorm, wg, bg, attn, four, wap, wfp, wo):
    B, S, D = x.shape
    tm = TM_MERGE
    return pl.pallas_call(
        _merge_kernel,
        grid=(B, S // tm),
        in_specs=[pl.BlockSpec((1, tm, D), lambda b, i: (b, i, 0)), _mod_spec(0), _mod_spec(1), _mod_spec(2),
                  _const_spec((1, D)),
                  _const_spec((D, N_BRANCH * D)), _const_spec((1, N_BRANCH * D)),
                  pl.BlockSpec((1, tm, Q_W), lambda b, i: (b, i, 0)),
                  pl.BlockSpec((1, tm, F_W), lambda b, i: (b, i, 0)),
                  _const_spec((Q_W, D)), _const_spec((F_W, D)), _const_spec((D, D))],
        out_specs=pl.BlockSpec((1, tm, D), lambda b, i: (b, i, 0)),
        out_shape=jax.ShapeDtypeStruct((B, S, D), F32),
        compiler_params=_params("parallel", "parallel"),
        name="merge",
    )(x, mods3, mods3, mods3, gnorm, wg, bg, attn, four, wap, wfp, wo)


def _ffn_kernel(x_ref, shift_ref, scale_ref, gate_ref, gnorm_ref, wgu_ref, wdown_ref, gfinal_ref, o_ref):
    x = x_ref[0]
    d_ff = wdown_ref.shape[0]
    h = _modulated_norm(x, gnorm_ref[...], shift_ref[0], scale_ref[0]).astype(BF16)
    ffn = None
    for lo in range(0, d_ff, FF_CHUNK):
        gate = jnp.dot(h, wgu_ref[:, lo:lo + FF_CHUNK], preferred_element_type=F32)
        up = jnp.dot(h, wgu_ref[:, d_ff + lo:d_ff + lo + FF_CHUNK], preferred_element_type=F32)
        act = (gate * _sigmoid(gate) * up).astype(BF16)
        part = jnp.dot(act, wdown_ref[lo:lo + FF_CHUNK, :], preferred_element_type=F32)
        ffn = part if ffn is None else ffn + part
    x2 = x + gate_ref[0] * ffn
    ms = jnp.mean(x2 * x2, axis=-1, keepdims=True)
    o_ref[0] = x2 * lax.rsqrt(ms + EPS) * gfinal_ref[...]


def _ffn(x, mods3, gnorm, wgu, wdown, gfinal):
    B, S, D = x.shape
    d_ff = wdown.shape[0]
    tm = TM_FFN
    return pl.pallas_call(
        _ffn_kernel,
        grid=(B, S // tm),
        in_specs=[pl.BlockSpec((1, tm, D), lambda b, i: (b, i, 0)), _mod_spec(3), _mod_spec(4), _mod_spec(5),
                  _const_spec((1, D)),
                  _const_spec((D, 2 * d_ff)), _const_spec((d_ff, D)), _const_spec((1, D))],
        out_specs=pl.BlockSpec((1, tm, D), lambda b, i: (b, i, 0)),
        out_shape=jax.ShapeDtypeStruct((B, S, D), F32),
        compiler_params=_params("parallel", "parallel"),
        name="ffn",
    )(x, mods3, mods3, mods3, gnorm, wgu, wdown, gfinal)


@functools.lru_cache(maxsize=None)
def _rope_tables_t(seq):
    rows = seq // GRID_W
    t_row = np.repeat(np.arange(rows, dtype=np.float64), GRID_W)
    t_col = np.tile(np.arange(GRID_W, dtype=np.float64), rows)
    inv_freq = ROPE_THETA ** (-np.arange(0, ROPE_AXIS_DIM, 2, dtype=np.float64) / ROPE_AXIS_DIM)
    ang = np.concatenate([t_row[:, None] * inv_freq, t_col[:, None] * inv_freq], axis=-1)
    return np.cos(ang).T.astype(np.float32), np.sin(ang).T.astype(np.float32)


def kernel(x, c, ctx, c_ctx, w_ada, b_ada, g_norm_mix, g_norm_ffn, w_in, b_gate, g_q, g_k, w_attn_proj,
           w_fourier_proj, w_o, w_gate_up, w_down, g_final):
    B, S, D = x.shape
    C = ctx.shape[1]
    assert D == D_MODEL and w_in.shape[0] == 1 and B < COND_ROWS
    assert S % TM_PROJ == 0 and TM_PROJ % TM_SUB == 0 and S % TM_MERGE == 0 and S % TM_FFN == 0 and S % TQ == 0
    assert w_down.shape[1] % FF_CHUNK == 0
    layer = 0

    cond = jnp.zeros((COND_ROWS, D), F32).at[:B].set(c).at[B].set(c_ctx)
    mods = _adaln(cond, w_ada[layer], b_ada[layer])
    mods3 = mods.reshape(COND_ROWS * N_MODS, 1, D)

    w_in_l = w_in[layer]
    wqkvT = w_in_l[:, :V_END].T.astype(BF16)
    wkvT = w_in_l[:, Q_W:V_END].T.astype(BF16)
    wf = w_in_l[:, V_END:F_END].astype(BF16)
    wg = w_in_l[:, F_END:].astype(BF16)
    gnorm_mix = g_norm_mix[layer].reshape(1, D)
    gnorm_ffn = g_norm_ffn[layer].reshape(1, D)

    gq_col = (g_q[layer] * (HEAD_DIM ** -0.5 * LOG2_E)).reshape(HEAD_DIM, 1)
    gk_col = g_k[layer].reshape(HEAD_DIM, 1)
    gq_b = jnp.broadcast_to(gq_col, (HEAD_DIM, TM_SUB))
    gk_b = jnp.broadcast_to(gk_col, (HEAD_DIM, TM_SUB))
    gk_ctx = jnp.broadcast_to(gk_col, (HEAD_DIM, C))
    cosT, sinT = (jnp.asarray(t) for t in _rope_tables_t(S))

    qT, k_all, vT_all, f = _in_proj(x, mods3, gnorm_mix, wqkvT, wf, gq_b, gk_b, cosT, sinT, C)
    k_all, vT_all = _ctx_kv(ctx, mods3, gnorm_mix, wkvT, gk_ctx, k_all, vT_all)
    attn = _attention(qT, k_all, vT_all)
    four = _fourier(f)
    x1 = _merge(x, mods3, gnorm_mix, wg, b_gate[layer].reshape(1, -1), attn, four,
                w_attn_proj[layer].astype(BF16), w_fourier_proj[layer].astype(BF16), w_o[layer].astype(BF16))
    return _ffn(x1, mods3, gnorm_ffn, w_gate_up[layer].astype(BF16), w_down[layer].astype(BF16),
                g_final.reshape(1, D))
```

```python
import functools

import numpy as np
import jax
import jax.numpy as jnp
from jax import lax
from jax.experimental import pallas as pl
from jax.experimental.pallas import tpu as pltpu

D_MODEL = 1024
GRID_W = 64
HEAD_DIM = 64
N_HEADS = 16
N_KV_HEADS = 4
GROUP = N_HEADS // N_KV_HEADS
ROPE_AXIS_DIM = HEAD_DIM // 2
ROPE_THETA = 10000.0
N_FGROUPS = 4
FGROUP_DIM = 128
N_BRANCH = 2
Q_W = N_HEADS * HEAD_DIM
KV_W = N_KV_HEADS * HEAD_DIM
F_W = N_FGROUPS * FGROUP_DIM
K_END = Q_W + KV_W
V_END = K_END + KV_W
F_END = V_END + F_W
EPS = 1e-6
LOG2_E = 1.4426950408889634

BF16 = jnp.bfloat16
F32 = jnp.float32

VMEM_LIMIT_BYTES = 56 * 1024 * 1024

COND_ROWS = 16
N_MODS = 6
TM_PROJ = 2048
TM_SUB = 512
TM_MERGE = 1024
TM_FFN = 1024
FF_CHUNK = 256
TQ = 512
KEY_CHUNK = 256


def _params(*sem):
    return pltpu.CompilerParams(dimension_semantics=sem, vmem_limit_bytes=VMEM_LIMIT_BYTES)


def _const_spec(shape):
    nd = len(shape)
    return pl.BlockSpec(shape, lambda *_: (0,) * nd, pipeline_mode=pl.Buffered(1))


def _mod_spec(k, row=None):
    if row is None:
        return pl.BlockSpec((1, 1, D_MODEL), lambda b, *_: (b * N_MODS + k, 0, 0))
    return pl.BlockSpec((1, 1, D_MODEL), lambda *_: (row * N_MODS + k, 0, 0))


def _sigmoid(x):
    return 1.0 / (1.0 + jnp.exp(-x))


def _modulated_norm(x, gain, shift, scale):
    ms = jnp.mean(x * x, axis=-1, keepdims=True)
    y = x * lax.rsqrt(ms + EPS) * gain
    return y * (1.0 + scale) + shift


def _head_norm_t(zh, gain):
    ms = jnp.mean(zh * zh, axis=0, keepdims=True)
    return zh * lax.rsqrt(ms + EPS) * gain


def _rope_t(zn, cos, sin):
    half = HEAD_DIM // 2
    x1, x2 = zn[:half], zn[half:]
    return x1 * cos - x2 * sin, x2 * cos + x1 * sin


def _adaln_kernel(cond_ref, w_ref, b_ref, o_ref):
    c = cond_ref[...]
    s = c * _sigmoid(c)
    o_ref[...] = jnp.dot(s, w_ref[...], preferred_element_type=F32,
                         precision=lax.Precision.HIGHEST) + b_ref[...]


def _adaln(cond, w, b):
    n = w.shape[1]
    tn = n // 4
    return pl.pallas_call(
        _adaln_kernel,
        grid=(n // tn,),
        in_specs=[_const_spec((COND_ROWS, D_MODEL)),
                  pl.BlockSpec((D_MODEL, tn), lambda j: (0, j)),
                  pl.BlockSpec((1, tn), lambda j: (0, j))],
        out_specs=pl.BlockSpec((COND_ROWS, tn), lambda j: (0, j)),
        out_shape=jax.ShapeDtypeStruct((COND_ROWS, n), F32),
        compiler_params=_params("parallel"),
        name="adaln",
    )(cond, w, b.reshape(1, n))


def _in_proj_kernel(x_ref, shift_ref, scale_ref, gnorm_ref, wqkvT_ref, wf_ref, gq_ref, gk_ref,
                    cos_ref, sin_ref, qT_ref, k_ref, vT_ref, f_ref):
    tm = x_ref.shape[1]
    gq, gk = gq_ref[...], gk_ref[...]
    half = HEAD_DIM // 2
    for j in range(tm // TM_SUB):
        tok = slice(j * TM_SUB, (j + 1) * TM_SUB)
        h = _modulated_norm(x_ref[0, tok, :], gnorm_ref[...], shift_ref[0], scale_ref[0]).astype(BF16)
        zT = lax.dot_general(wqkvT_ref[...], h, (((1,), (1,)), ((), ())), preferred_element_type=F32)
        cos, sin = cos_ref[:, tok], sin_ref[:, tok]
        for i in range(N_HEADS):
            lo = i * HEAD_DIM
            a, b = _rope_t(_head_norm_t(zT[lo:lo + HEAD_DIM], gq), cos, sin)
            qT_ref[0, lo:lo + half, tok] = a.astype(BF16)
            qT_ref[0, lo + half:lo + HEAD_DIM, tok] = b.astype(BF16)
        k_parts = []
        for i in range(N_KV_HEADS):
            lo = Q_W + i * HEAD_DIM
            k_parts.extend(_rope_t(_head_norm_t(zT[lo:lo + HEAD_DIM], gk), cos, sin))
        k_tok = jnp.concatenate(k_parts, axis=0).T
        for g in range(N_KV_HEADS):
            k_ref[0, g, tok, :] = k_tok[:, g * HEAD_DIM:(g + 1) * HEAD_DIM].astype(BF16)
        vT_ref[0, :, tok] = zT[K_END:V_END].astype(BF16)
        f_ref[0, tok, :] = jnp.dot(h, wf_ref[...], preferred_element_type=F32).astype(BF16)


def _in_proj(x, mods3, gnorm, wqkvT, wf, gq_b, gk_b, cosT, sinT, n_ctx):
    B, S, D = x.shape
    tm = TM_PROJ
    half = HEAD_DIM // 2
    return pl.pallas_call(
        _in_proj_kernel,
        grid=(B, S // tm),
        in_specs=[pl.BlockSpec((1, tm, D), lambda b, i: (b, i, 0)),
                  _mod_spec(0), _mod_spec(1),
                  _const_spec((1, D)),
                  _const_spec((V_END, D)), _const_spec((D, F_W)),
                  _const_spec((HEAD_DIM, TM_SUB)), _const_spec((HEAD_DIM, TM_SUB)),
                  pl.BlockSpec((half, tm), lambda b, i: (0, i)),
                  pl.BlockSpec((half, tm), lambda b, i: (0, i))],
        out_specs=[pl.BlockSpec((1, Q_W, tm), lambda b, i: (b, 0, i)),
                   pl.BlockSpec((1, N_KV_HEADS, tm, HEAD_DIM), lambda b, i: (b, 0, i, 0)),
                   pl.BlockSpec((1, KV_W, tm), lambda b, i: (b, 0, i)),
                   pl.BlockSpec((1, tm, F_W), lambda b, i: (b, i, 0))],
        out_shape=[jax.ShapeDtypeStruct((B, Q_W, S), BF16),
                   jax.ShapeDtypeStruct((B, N_KV_HEADS, S + n_ctx, HEAD_DIM), BF16),
                   jax.ShapeDtypeStruct((B, KV_W, S + n_ctx), BF16),
                   jax.ShapeDtypeStruct((B, S, F_W), BF16)],
        compiler_params=_params("parallel", "parallel"),
        name="in_proj",
    )(x, mods3, mods3, gnorm, wqkvT, wf, gq_b, gk_b, cosT, sinT)


def _ctx_kv_kernel(x_ref, shift_ref, scale_ref, gnorm_ref, wkvT_ref, gk_ref, k_in_ref, vT_in_ref,
                   k_ref, vT_ref):
    del k_in_ref, vT_in_ref
    h = _modulated_norm(x_ref[0], gnorm_ref[...], shift_ref[0], scale_ref[0]).astype(BF16)
    zT = lax.dot_general(wkvT_ref[...], h, (((1,), (1,)), ((), ())), preferred_element_type=F32)
    gk = gk_ref[...]
    kT = jnp.concatenate(
        [_head_norm_t(zT[i * HEAD_DIM:(i + 1) * HEAD_DIM], gk) for i in range(N_KV_HEADS)], axis=0)
    k_tok = kT.T
    for g in range(N_KV_HEADS):
        k_ref[0, g] = k_tok[:, g * HEAD_DIM:(g + 1) * HEAD_DIM].astype(BF16)
    vT_ref[0] = zT[KV_W:].astype(BF16)


def _ctx_kv(ctx, mods3, gnorm, wkvT, gk_b, k_all, vT_all):
    B, C, D = ctx.shape
    n_lat = k_all.shape[2] - C
    assert n_lat % C == 0
    blk = n_lat // C
    return pl.pallas_call(
        _ctx_kv_kernel,
        grid=(B,),
        in_specs=[pl.BlockSpec((1, C, D), lambda b: (b, 0, 0)),
                  _mod_spec(0, row=B), _mod_spec(1, row=B), _const_spec((1, D)),
                  _const_spec((2 * KV_W, D)), _const_spec((HEAD_DIM, C)),
                  pl.BlockSpec(memory_space=pl.ANY), pl.BlockSpec(memory_space=pl.ANY)],
        out_specs=[pl.BlockSpec((1, N_KV_HEADS, C, HEAD_DIM), lambda b: (b, 0, blk, 0)),
                   pl.BlockSpec((1, KV_W, C), lambda b: (b, 0, blk))],
        out_shape=[jax.ShapeDtypeStruct(k_all.shape, BF16),
                   jax.ShapeDtypeStruct(vT_all.shape, BF16)],
        input_output_aliases={6: 0, 7: 1},
        compiler_params=_params("parallel"),
        name="ctx_kv",
    )(ctx, mods3, mods3, gnorm, wkvT, gk_b, k_all, vT_all)


def _attention_kernel(qT_ref, qn_ref, k_ref, kn_ref, vT_ref, o_ref, vc_ref, s0_ref, s1_ref, m0_ref, m1_ref, oT_ref):
    first = jnp.logical_and(pl.program_id(0) == 0, pl.program_id(1) == 0)
    tq = qT_ref.shape[2]
    n_chunks = k_ref.shape[2] // KEY_CHUNK

    @pl.when(pl.program_id(1) == 0)
    def _():
        for g in range(N_KV_HEADS):
            for c in range(n_chunks):
                vc_ref[g, c] = vT_ref[0, g * HEAD_DIM:(g + 1) * HEAD_DIM, c * KEY_CHUNK:(c + 1) * KEY_CHUNK]

    def head_rows(t):
        return slice(t * HEAD_DIM, (t + 1) * HEAD_DIM)

    def step(t_b, sb, sa, a_keys=None, a_q=None):
        m8 = jnp.full((8, tq), -jnp.inf, F32)
        l8 = jnp.zeros((8, tq), F32)
        o = None
        for c in range(n_chunks):
            rows = slice(c * KEY_CHUNK, (c + 1) * KEY_CHUNK)
            srows = pl.ds(pl.multiple_of(row0 + c * KEY_CHUNK, KEY_CHUNK), KEY_CHUNK)
            if sb is not None:
                e = jnp.exp2(sb[0][srows, :] - sb[1][0:1, :])
                l8 = l8 + jnp.sum(e.reshape(KEY_CHUNK // 8, 8, tq), axis=0)
                d = jnp.dot(vc_ref[t_b // GROUP, c], e.astype(BF16), preferred_element_type=F32)
                o = d if o is None else o + d
            if sa is not None:
                s = jnp.dot(a_keys(rows), a_q(), preferred_element_type=F32)
                sa[0][srows, :] = s
                m8 = jnp.maximum(m8, jnp.max(s.reshape(KEY_CHUNK // 8, 8, tq), axis=0))
        if sa is not None:
            sa[1][...] = jnp.broadcast_to(jnp.max(m8, axis=0, keepdims=True), (8, tq))
        if sb is not None:
            oT_ref[head_rows(t_b), :] = o / jnp.sum(l8, axis=0, keepdims=True)

    def cur(t):
        return (lambda rows: k_ref[0, t // GROUP, rows, :]), (lambda: qT_ref[0, head_rows(t), :])

    bufs = ((s0_ref, m0_ref), (s1_ref, m1_ref))
    row0 = jnp.minimum(pl.program_id(0), 0)

    @pl.when(first)
    def _():
        step(-1, None, bufs[0], *cur(0))

    for t in range(N_HEADS - 1):
        step(t, bufs[t % 2], bufs[(t + 1) % 2], *cur(t + 1))
    step(N_HEADS - 1, bufs[(N_HEADS - 1) % 2], bufs[N_HEADS % 2],
         lambda rows: kn_ref[0, 0, rows, :], lambda: qn_ref[0])
    o_ref[0] = oT_ref[...].T.astype(BF16)


def _attention(qT, k_all, vT_all):
    B, _, S = qT.shape
    T = k_all.shape[2]
    assert T % KEY_CHUNK == 0 and N_HEADS % 2 == 0
    n_chunks = T // KEY_CHUNK
    nq = S // TQ

    def nxt(b, i):
        wrap = i + 1 == nq
        return jnp.where(wrap, jnp.minimum(b + 1, B - 1), b), jnp.where(wrap, 0, i + 1)

    return pl.pallas_call(
        _attention_kernel,
        grid=(B, nq),
        in_specs=[pl.BlockSpec((1, Q_W, TQ), lambda b, i: (b, 0, i)),
                  pl.BlockSpec((1, HEAD_DIM, TQ), lambda b, i: (nxt(b, i)[0], 0, nxt(b, i)[1])),
                  pl.BlockSpec((1, N_KV_HEADS, T, HEAD_DIM), lambda b, i: (b, 0, 0, 0)),
                  pl.BlockSpec((1, 1, T, HEAD_DIM), lambda b, i: (nxt(b, i)[0], 0, 0, 0)),
                  pl.BlockSpec((1, KV_W, T), lambda b, i: (b, 0, 0))],
        out_specs=pl.BlockSpec((1, TQ, Q_W), lambda b, i: (b, i, 0)),
        out_shape=jax.ShapeDtypeStruct((B, S, Q_W), BF16),
        scratch_shapes=[pltpu.VMEM((N_KV_HEADS, n_chunks, HEAD_DIM, KEY_CHUNK), BF16),
                        pltpu.VMEM((T, TQ), F32), pltpu.VMEM((T, TQ), F32),
                        pltpu.VMEM((8, TQ), F32), pltpu.VMEM((8, TQ), F32),
                        pltpu.VMEM((Q_W, TQ), F32)],
        compiler_params=_params("arbitrary", "arbitrary"),
        name="attention",
    )(qT, qT, k_all, k_all, vT_all)


POS_RADIX = 4


@functools.lru_cache(maxsize=None)
def _dft_tables(n_pos):
    def cos_sin(rows, cols, period):
        ang = 2.0 * np.pi * ((rows[:, None] * cols[None, :]) % period).astype(np.float64) / period
        return np.cos(ang), np.sin(ang)
    q = n_pos // POS_RADIX
    ch = np.arange(FGROUP_DIM, dtype=np.int64)
    cc, sc = cos_sin(ch, ch, FGROUP_DIM)
    n2 = np.arange(q, dtype=np.int64)
    cq, sq = cos_sin(n2, n2, q)
    tc, ts = cos_sin(np.arange(POS_RADIX, dtype=np.int64), n2, n_pos)
    lanes = np.ones((1, F_W))
    chan = np.concatenate([cc, sc], axis=1).astype(np.float32)
    pos = np.concatenate([cq, sq], axis=1).astype(np.float32)
    twc = (tc.reshape(n_pos, 1) * lanes).astype(np.float32)
    tws = (ts.reshape(n_pos, 1) * lanes).astype(np.float32)
    return chan, pos, twc, tws


def _fourier_kernel(f_ref, chan_ref, pos_ref, twc_ref, tws_ref, o_ref, ur_ref, ui_ref, t_ref, y_ref, *, scale):
    assert POS_RADIX == 4
    n_pos = f_ref.shape[1]
    q = n_pos // POS_RADIX
    for g in range(N_FGROUPS):
        cols = slice(g * FGROUP_DIM, (g + 1) * FGROUP_DIM)
        xcs = jnp.dot(f_ref[0, :, cols], chan_ref[...], preferred_element_type=F32)
        ur_ref[:, cols] = xcs[:, :FGROUP_DIM]
        ui_ref[:, cols] = -xcs[:, FGROUP_DIM:]
    ur = [ur_ref[i * q:(i + 1) * q, :] for i in range(POS_RADIX)]
    ui = [ui_ref[i * q:(i + 1) * q, :] for i in range(POS_RADIX)]
    er, ei = ur[0] + ur[2], ui[0] + ui[2]
    sr, si = ur[1] + ur[3], ui[1] + ui[3]
    dr, di = ur[0] - ur[2], ui[0] - ui[2]
    gr, gi = ur[1] - ur[3], ui[1] - ui[3]
    butterflies = [(er + sr, ei + si), (dr + gi, di - gr), (er - sr, ei - si), (dr - gi, di + gr)]
    for k1, (tr, ti) in enumerate(butterflies):
        if k1:
            c = twc_ref[k1 * q:(k1 + 1) * q, :]
            s = tws_ref[k1 * q:(k1 + 1) * q, :]
            tr, ti = tr * c + ti * s, ti * c - tr * s
        t_ref[k1, 0:q, :] = tr.astype(BF16)
        t_ref[k1, q:2 * q, :] = ti.astype(BF16)
    for k1 in range(POS_RADIX):
        y = jnp.dot(pos_ref[...], t_ref[k1], preferred_element_type=F32) * scale
        for g in range(N_FGROUPS):
            y_ref[g, pl.ds(k1, q, stride=POS_RADIX), :] = y[:, g * FGROUP_DIM:(g + 1) * FGROUP_DIM]
    o_ref[0] = jnp.concatenate([y_ref[g] for g in range(N_FGROUPS)], axis=1).astype(BF16)


def _fourier(f):
    B, S, W = f.shape
    assert S % POS_RADIX == 0
    q = S // POS_RADIX
    chan, pos, twc, tws = _dft_tables(S)
    scale = float(1.0 / np.sqrt(S * FGROUP_DIM))
    return pl.pallas_call(
        functools.partial(_fourier_kernel, scale=scale),
        grid=(B,),
        in_specs=[pl.BlockSpec((1, S, W), lambda b: (b, 0, 0)),
                  _const_spec((FGROUP_DIM, 2 * FGROUP_DIM)),
                  _const_spec((q, 2 * q)),
                  _const_spec((S, W)), _const_spec((S, W))],
        out_specs=pl.BlockSpec((1, S, W), lambda b: (b, 0, 0)),
        out_shape=jax.ShapeDtypeStruct((B, S, W), BF16),
        scratch_shapes=[pltpu.VMEM((S, W), F32), pltpu.VMEM((S, W), F32),
                        pltpu.VMEM((POS_RADIX, 2 * q, W), BF16),
                        pltpu.VMEM((N_FGROUPS, S, FGROUP_DIM), F32)],
        compiler_params=_params("parallel"),
        name="fourier",
    )(f, jnp.asarray(chan).astype(BF16), jnp.asarray(pos).astype(BF16), jnp.asarray(twc), jnp.asarray(tws))


def _merge_kernel(x_ref, shift_ref, scale_ref, gate_ref, gnorm_ref, wg_ref, bg_ref, attn_ref, four_ref,
                  wap_ref, wfp_ref, wo_ref, o_ref):
    x = x_ref[0]
    h = _modulated_norm(x, gnorm_ref[...], shift_ref[0], scale_ref[0]).astype(BF16)
    gates = _sigmoid(jnp.dot(h, wg_ref[...], preferred_element_type=F32) + bg_ref[...])
    a = jnp.dot(attn_ref[0], wap_ref[...], preferred_element_type=F32)
    fo = jnp.dot(four_ref[0], wfp_ref[...], preferred_element_type=F32)
    merged = gates[:, :D_MODEL] * a + gates[:, D_MODEL:] * fo
    mix = jnp.dot(merged.astype(BF16), wo_ref[...], preferred_element_type=F32)
    o_ref[0] = x + gate_ref[0] * mix


def _merge(x, mods3, gnorm, wg, bg, attn, four, wap, wfp, wo):
    B, S, D = x.shape
    tm = TM_MERGE
    return pl.pallas_call(
        _merge_kernel,
        grid=(B, S // tm),
        in_specs=[pl.BlockSpec((1, tm, D), lambda b, i: (b, i, 0)), _mod_spec(0), _mod_spec(1), _mod_spec(2),
                  _const_spec((1, D)),
                  _const_spec((D, N_BRANCH * D)), _const_spec((1, N_BRANCH * D)),
                  pl.BlockSpec((1, tm, Q_W), lambda b, i: (b, i, 0)),
                  pl.BlockSpec((1, tm, F_W), lambda b, i: (b, i, 0)),
                  _const_spec((Q_W, D)), _const_spec((F_W, D)), _const_spec((D, D))],
        out_specs=pl.BlockSpec((1, tm, D), lambda b, i: (b, i, 0)),
        out_shape=jax.ShapeDtypeStruct((B, S, D), F32),
        compiler_params=_params("parallel", "parallel"),
        name="merge",
    )(x, mods3, mods3, mods3, gnorm, wg, bg, attn, four, wap, wfp, wo)


def _ffn_kernel(x_ref, shift_ref, scale_ref, gate_ref, gnorm_ref, wgu_ref, wdown_ref, gfinal_ref, o_ref):
    x = x_ref[0]
    d_ff = wdown_ref.shape[0]
    h = _modulated_norm(x, gnorm_ref[...], shift_ref[0], scale_ref[0]).astype(BF16)
    ffn = None
    for lo in range(0, d_ff, FF_CHUNK):
        gate = jnp.dot(h, wgu_ref[:, lo:lo + FF_CHUNK], preferred_element_type=F32)
        up = jnp.dot(h, wgu_ref[:, d_ff + lo:d_ff + lo + FF_CHUNK], preferred_element_type=F32)
        act = (gate * _sigmoid(gate) * up).astype(BF16)
        part = jnp.dot(act, wdown_ref[lo:lo + FF_CHUNK, :], preferred_element_type=F32)
        ffn = part if ffn is None else ffn + part
    x2 = x + gate_ref[0] * ffn
    ms = jnp.mean(x2 * x2, axis=-1, keepdims=True)
    o_ref[0] = x2 * lax.rsqrt(ms + EPS) * gfinal_ref[...]


def _ffn(x, mods3, gnorm, wgu, wdown, gfinal):
    B, S, D = x.shape
    d_ff = wdown.shape[0]
    tm = TM_FFN
    return pl.pallas_call(
        _ffn_kernel,
        grid=(B, S // tm),
        in_specs=[pl.BlockSpec((1, tm, D), lambda b, i: (b, i, 0)), _mod_spec(3), _mod_spec(4), _mod_spec(5),
                  _const_spec((1, D)),
                  _const_spec((D, 2 * d_ff)), _const_spec((d_ff, D)), _const_spec((1, D))],
        out_specs=pl.BlockSpec((1, tm, D), lambda b, i: (b, i, 0)),
        out_shape=jax.ShapeDtypeStruct((B, S, D), F32),
        compiler_params=_params("parallel", "parallel"),
        name="ffn",
    )(x, mods3, mods3, mods3, gnorm, wgu, wdown, gfinal)


@functools.lru_cache(maxsize=None)
def _rope_tables_t(seq):
    rows = seq // GRID_W
    t_row = np.repeat(np.arange(rows, dtype=np.float64), GRID_W)
    t_col = np.tile(np.arange(GRID_W, dtype=np.float64), rows)
    inv_freq = ROPE_THETA ** (-np.arange(0, ROPE_AXIS_DIM, 2, dtype=np.float64) / ROPE_AXIS_DIM)
    ang = np.concatenate([t_row[:, None] * inv_freq, t_col[:, None] * inv_freq], axis=-1)
    return np.cos(ang).T.astype(np.float32), np.sin(ang).T.astype(np.float32)


def kernel(x, c, ctx, c_ctx, w_ada, b_ada, g_norm_mix, g_norm_ffn, w_in, b_gate, g_q, g_k, w_attn_proj,
           w_fourier_proj, w_o, w_gate_up, w_down, g_final):
    B, S, D = x.shape
    C = ctx.shape[1]
    assert D == D_MODEL and w_in.shape[0] == 1 and B < COND_ROWS
    assert S % TM_PROJ == 0 and TM_PROJ % TM_SUB == 0 and S % TM_MERGE == 0 and S % TM_FFN == 0 and S % TQ == 0
    assert w_down.shape[1] % FF_CHUNK == 0
    layer = 0

    cond = jnp.zeros((COND_ROWS, D), F32).at[:B].set(c).at[B].set(c_ctx)
    mods = _adaln(cond, w_ada[layer], b_ada[layer])
    mods3 = mods.reshape(COND_ROWS * N_MODS, 1, D)

    w_in_l = w_in[layer]
    wqkvT = w_in_l[:, :V_END].T.astype(BF16)
    wkvT = w_in_l[:, Q_W:V_END].T.astype(BF16)
    wf = w_in_l[:, V_END:F_END].astype(BF16)
    wg = w_in_l[:, F_END:].astype(BF16)
    gnorm_mix = g_norm_mix[layer].reshape(1, D)
    gnorm_ffn = g_norm_ffn[layer].reshape(1, D)

    gq_col = (g_q[layer] * (HEAD_DIM ** -0.5 * LOG2_E)).reshape(HEAD_DIM, 1)
    gk_col = g_k[layer].reshape(HEAD_DIM, 1)
    gq_b = jnp.broadcast_to(gq_col, (HEAD_DIM, TM_SUB))
    gk_b = jnp.broadcast_to(gk_col, (HEAD_DIM, TM_SUB))
    gk_ctx = jnp.broadcast_to(gk_col, (HEAD_DIM, C))
    cosT, sinT = (jnp.asarray(t) for t in _rope_tables_t(S))

    qT, k_all, vT_all, f = _in_proj(x, mods3, gnorm_mix, wqkvT, wf, gq_b, gk_b, cosT, sinT, C)
    k_all, vT_all = _ctx_kv(ctx, mods3, gnorm_mix, wkvT, gk_ctx, k_all, vT_all)
    attn = _attention(qT, k_all, vT_all)
    four = _fourier(f)
    x1 = _merge(x, mods3, gnorm_mix, wg, b_gate[layer].reshape(1, -1), attn, four,
                w_attn_proj[layer].astype(BF16), w_fourier_proj[layer].astype(BF16), w_o[layer].astype(BF16))
    return _ffn(x1, mods3, gnorm_ffn, w_gate_up[layer].astype(BF16), w_down[layer].astype(BF16),
                g_final.reshape(1, D))
```

```python
import functools

import numpy as np
import jax
import jax.numpy as jnp
from jax import lax
from jax.experimental import pallas as pl
from jax.experimental.pallas import tpu as pltpu

D_MODEL = 1024
GRID_W = 64
HEAD_DIM = 64
N_HEADS = 16
N_KV_HEADS = 4
GROUP = N_HEADS // N_KV_HEADS
ROPE_AXIS_DIM = HEAD_DIM // 2
ROPE_THETA = 10000.0
N_FGROUPS = 4
FGROUP_DIM = 128
N_BRANCH = 2
Q_W = N_HEADS * HEAD_DIM
KV_W = N_KV_HEADS * HEAD_DIM
F_W = N_FGROUPS * FGROUP_DIM
K_END = Q_W + KV_W
V_END = K_END + KV_W
F_END = V_END + F_W
EPS = 1e-6
LOG2_E = 1.4426950408889634

BF16 = jnp.bfloat16
F32 = jnp.float32

VMEM_LIMIT_BYTES = 56 * 1024 * 1024

COND_ROWS = 16
N_MODS = 6
ADALN_COLS = 512
CTX_BATCH = 4
TM_PROJ = 2048
TM_SUB = 512
TM_MERGE = 1024
TM_FFN = 1024
FF_CHUNK = 256
TQ = 512
KEY_CHUNK = 256


def _params(*sem):
    return pltpu.CompilerParams(dimension_semantics=sem, vmem_limit_bytes=VMEM_LIMIT_BYTES)


def _const_spec(shape):
    nd = len(shape)
    return pl.BlockSpec(shape, lambda *_: (0,) * nd, pipeline_mode=pl.Buffered(1))


def _mod_spec(k, row=None):
    if row is None:
        return pl.BlockSpec((1, 1, D_MODEL), lambda b, *_: (b * N_MODS + k, 0, 0))
    return pl.BlockSpec((1, 1, D_MODEL), lambda *_: (row * N_MODS + k, 0, 0))


def _sigmoid(x):
    return 1.0 / (1.0 + jnp.exp(-x))


def _modulated_norm(x, gain, shift, scale):
    ms = jnp.mean(x * x, axis=-1, keepdims=True)
    y = x * lax.rsqrt(ms + EPS) * gain
    return y * (1.0 + scale) + shift


def _head_norm_t(zh, gain):
    ms = jnp.mean(zh * zh, axis=0, keepdims=True)
    return zh * lax.rsqrt(ms + EPS) * gain


def _rope_t(zn, cos, sin):
    half = HEAD_DIM // 2
    x1, x2 = zn[:half], zn[half:]
    return x1 * cos - x2 * sin, x2 * cos + x1 * sin


def _adaln_kernel(cond_ref, w_ref, b_ref, o_ref):
    c = cond_ref[...]
    s = c * _sigmoid(c)
    o_ref[...] = jnp.dot(s, w_ref[...], preferred_element_type=F32,
                         precision=lax.Precision.HIGHEST) + b_ref[...]


def _adaln(cond, w, b):
    n = w.shape[1]
    tn = ADALN_COLS
    assert n % tn == 0
    return pl.pallas_call(
        _adaln_kernel,
        grid=(n // tn,),
        in_specs=[_const_spec((COND_ROWS, D_MODEL)),
                  pl.BlockSpec((D_MODEL, tn), lambda j: (0, j)),
                  pl.BlockSpec((1, tn), lambda j: (0, j))],
        out_specs=pl.BlockSpec((COND_ROWS, tn), lambda j: (0, j)),
        out_shape=jax.ShapeDtypeStruct((COND_ROWS, n), F32),
        compiler_params=_params("parallel"),
        name="adaln",
    )(cond, w, b.reshape(1, n))


def _in_proj_kernel(x_ref, shift_ref, scale_ref, gnorm_ref, wqkvT_ref, wf_ref, gq_ref, gk_ref,
                    cos_ref, sin_ref, qT_ref, k_ref, vT_ref, f_ref):
    tm = x_ref.shape[1]
    gq, gk = gq_ref[...], gk_ref[...]
    half = HEAD_DIM // 2
    for j in range(tm // TM_SUB):
        tok = slice(j * TM_SUB, (j + 1) * TM_SUB)
        h = _modulated_norm(x_ref[0, tok, :], gnorm_ref[...], shift_ref[0], scale_ref[0]).astype(BF16)
        zT = lax.dot_general(wqkvT_ref[...], h, (((1,), (1,)), ((), ())), preferred_element_type=F32)
        cos, sin = cos_ref[:, tok], sin_ref[:, tok]
        for i in range(N_HEADS):
            lo = i * HEAD_DIM
            a, b = _rope_t(_head_norm_t(zT[lo:lo + HEAD_DIM], gq), cos, sin)
            qT_ref[0, lo:lo + half, tok] = a.astype(BF16)
            qT_ref[0, lo + half:lo + HEAD_DIM, tok] = b.astype(BF16)
        k_parts = []
        for i in range(N_KV_HEADS):
            lo = Q_W + i * HEAD_DIM
            k_parts.extend(_rope_t(_head_norm_t(zT[lo:lo + HEAD_DIM], gk), cos, sin))
        k_tok = jnp.concatenate(k_parts, axis=0).T
        for g in range(N_KV_HEADS):
            k_ref[0, g, tok, :] = k_tok[:, g * HEAD_DIM:(g + 1) * HEAD_DIM].astype(BF16)
        vT_ref[0, :, tok] = zT[K_END:V_END].astype(BF16)
        f_ref[0, tok, :] = jnp.dot(h, wf_ref[...], preferred_element_type=F32).astype(BF16)


def _in_proj(x, mods3, gnorm, wqkvT, wf, gq_b, gk_b, cosT, sinT, n_ctx):
    B, S, D = x.shape
    tm = TM_PROJ
    half = HEAD_DIM // 2
    return pl.pallas_call(
        _in_proj_kernel,
        grid=(B, S // tm),
        in_specs=[pl.BlockSpec((1, tm, D), lambda b, i: (b, i, 0)),
                  _mod_spec(0), _mod_spec(1),
                  _const_spec((1, D)),
                  _const_spec((V_END, D)), _const_spec((D, F_W)),
                  _const_spec((HEAD_DIM, TM_SUB)), _const_spec((HEAD_DIM, TM_SUB)),
                  pl.BlockSpec((half, tm), lambda b, i: (0, i)),
                  pl.BlockSpec((half, tm), lambda b, i: (0, i))],
        out_specs=[pl.BlockSpec((1, Q_W, tm), lambda b, i: (b, 0, i)),
                   pl.BlockSpec((1, N_KV_HEADS, tm, HEAD_DIM), lambda b, i: (b, 0, i, 0)),
                   pl.BlockSpec((1, KV_W, tm), lambda b, i: (b, 0, i)),
                   pl.BlockSpec((1, tm, F_W), lambda b, i: (b, i, 0))],
        out_shape=[jax.ShapeDtypeStruct((B, Q_W, S), BF16),
                   jax.ShapeDtypeStruct((B, N_KV_HEADS, S + n_ctx, HEAD_DIM), BF16),
                   jax.ShapeDtypeStruct((B, KV_W, S + n_ctx), BF16),
                   jax.ShapeDtypeStruct((B, S, F_W), BF16)],
        compiler_params=_params("parallel", "parallel"),
        name="in_proj",
    )(x, mods3, mods3, gnorm, wqkvT, wf, gq_b, gk_b, cosT, sinT)


def _ctx_kv_kernel(x_ref, shift_ref, scale_ref, gnorm_ref, wkvT_ref, gk_ref, k_in_ref, vT_in_ref,
                   k_ref, vT_ref):
    del k_in_ref, vT_in_ref
    gk = gk_ref[...]
    for i in range(x_ref.shape[0]):
        h = _modulated_norm(x_ref[i], gnorm_ref[...], shift_ref[0], scale_ref[0]).astype(BF16)
        zT = lax.dot_general(wkvT_ref[...], h, (((1,), (1,)), ((), ())), preferred_element_type=F32)
        kT = jnp.concatenate(
            [_head_norm_t(zT[j * HEAD_DIM:(j + 1) * HEAD_DIM], gk) for j in range(N_KV_HEADS)], axis=0)
        k_tok = kT.T
        for g in range(N_KV_HEADS):
            k_ref[i, g] = k_tok[:, g * HEAD_DIM:(g + 1) * HEAD_DIM].astype(BF16)
        vT_ref[i] = zT[KV_W:].astype(BF16)


def _ctx_kv(ctx, mods3, gnorm, wkvT, gk_b, k_all, vT_all):
    B, C, D = ctx.shape
    n_lat = k_all.shape[2] - C
    assert n_lat % C == 0 and B % CTX_BATCH == 0
    blk = n_lat // C
    nb = CTX_BATCH
    return pl.pallas_call(
        _ctx_kv_kernel,
        grid=(B // nb,),
        in_specs=[pl.BlockSpec((nb, C, D), lambda b: (b, 0, 0)),
                  _mod_spec(0, row=B), _mod_spec(1, row=B), _const_spec((1, D)),
                  _const_spec((2 * KV_W, D)), _const_spec((HEAD_DIM, C)),
                  pl.BlockSpec(memory_space=pl.ANY), pl.BlockSpec(memory_space=pl.ANY)],
        out_specs=[pl.BlockSpec((nb, N_KV_HEADS, C, HEAD_DIM), lambda b: (b, 0, blk, 0)),
                   pl.BlockSpec((nb, KV_W, C), lambda b: (b, 0, blk))],
        out_shape=[jax.ShapeDtypeStruct(k_all.shape, BF16),
                   jax.ShapeDtypeStruct(vT_all.shape, BF16)],
        input_output_aliases={6: 0, 7: 1},
        compiler_params=_params("parallel"),
        name="ctx_kv",
    )(ctx, mods3, mods3, gnorm, wkvT, gk_b, k_all, vT_all)


def _attention_kernel(qT_ref, qn_ref, k_ref, kn_ref, vT_ref, o_ref, vc_ref, s0_ref, s1_ref, m0_ref, m1_ref, oT_ref):
    first = jnp.logical_and(pl.program_id(0) == 0, pl.program_id(1) == 0)
    tq = qT_ref.shape[2]
    n_chunks = k_ref.shape[2] // KEY_CHUNK

    @pl.when(pl.program_id(1) == 0)
    def _():
        for g in range(N_KV_HEADS):
            for c in range(n_chunks):
                vc_ref[g, c] = vT_ref[0, g * HEAD_DIM:(g + 1) * HEAD_DIM, c * KEY_CHUNK:(c + 1) * KEY_CHUNK]

    def head_rows(t):
        return slice(t * HEAD_DIM, (t + 1) * HEAD_DIM)

    def step(t_b, sb, sa, a_keys=None, a_q=None):
        m8 = jnp.full((8, tq), -jnp.inf, F32)
        l8 = jnp.zeros((8, tq), F32)
        o = None
        for c in range(n_chunks):
            rows = slice(c * KEY_CHUNK, (c + 1) * KEY_CHUNK)
            srows = pl.ds(pl.multiple_of(row0 + c * KEY_CHUNK, KEY_CHUNK), KEY_CHUNK)
            if sb is not None:
                e = jnp.exp2(sb[0][srows, :] - sb[1][0:1, :])
                l8 = l8 + jnp.sum(e.reshape(KEY_CHUNK // 8, 8, tq), axis=0)
                d = jnp.dot(vc_ref[t_b // GROUP, c], e.astype(BF16), preferred_element_type=F32)
                o = d if o is None else o + d
            if sa is not None:
                s = jnp.dot(a_keys(rows), a_q(), preferred_element_type=F32)
                sa[0][srows, :] = s
                m8 = jnp.maximum(m8, jnp.max(s.reshape(KEY_CHUNK // 8, 8, tq), axis=0))
        if sa is not None:
            sa[1][...] = jnp.broadcast_to(jnp.max(m8, axis=0, keepdims=True), (8, tq))
        if sb is not None:
            oT_ref[head_rows(t_b), :] = o / jnp.sum(l8, axis=0, keepdims=True)

    def cur(t):
        return (lambda rows: k_ref[0, t // GROUP, rows, :]), (lambda: qT_ref[0, head_rows(t), :])

    bufs = ((s0_ref, m0_ref), (s1_ref, m1_ref))
    row0 = jnp.minimum(pl.program_id(0), 0)

    @pl.when(first)
    def _():
        step(-1, None, bufs[0], *cur(0))

    for t in range(N_HEADS - 1):
        step(t, bufs[t % 2], bufs[(t + 1) % 2], *cur(t + 1))
    step(N_HEADS - 1, bufs[(N_HEADS - 1) % 2], bufs[N_HEADS % 2],
         lambda rows: kn_ref[0, 0, rows, :], lambda: qn_ref[0])
    o_ref[0] = oT_ref[...].T.astype(BF16)


def _attention(qT, k_all, vT_all):
    B, _, S = qT.shape
    T = k_all.shape[2]
    assert T % KEY_CHUNK == 0 and N_HEADS % 2 == 0
    n_chunks = T // KEY_CHUNK
    nq = S // TQ

    def nxt(b, i):
        wrap = i + 1 == nq
        return jnp.where(wrap, jnp.minimum(b + 1, B - 1), b), jnp.where(wrap, 0, i + 1)

    return pl.pallas_call(
        _attention_kernel,
        grid=(B, nq),
        in_specs=[pl.BlockSpec((1, Q_W, TQ), lambda b, i: (b, 0, i)),
                  pl.BlockSpec((1, HEAD_DIM, TQ), lambda b, i: (nxt(b, i)[0], 0, nxt(b, i)[1])),
                  pl.BlockSpec((1, N_KV_HEADS, T, HEAD_DIM), lambda b, i: (b, 0, 0, 0)),
                  pl.BlockSpec((1, 1, T, HEAD_DIM), lambda b, i: (nxt(b, i)[0], 0, 0, 0)),
                  pl.BlockSpec((1, KV_W, T), lambda b, i: (b, 0, 0))],
        out_specs=pl.BlockSpec((1, TQ, Q_W), lambda b, i: (b, i, 0)),
        out_shape=jax.ShapeDtypeStruct((B, S, Q_W), BF16),
        scratch_shapes=[pltpu.VMEM((N_KV_HEADS, n_chunks, HEAD_DIM, KEY_CHUNK), BF16),
                        pltpu.VMEM((T, TQ), F32), pltpu.VMEM((T, TQ), F32),
                        pltpu.VMEM((8, TQ), F32), pltpu.VMEM((8, TQ), F32),
                        pltpu.VMEM((Q_W, TQ), F32)],
        compiler_params=_params("arbitrary", "arbitrary"),
        name="attention",
    )(qT, qT, k_all, k_all, vT_all)


POS_RADIX = 4


@functools.lru_cache(maxsize=None)
def _dft_tables(n_pos):
    def cos_sin(rows, cols, period):
        ang = 2.0 * np.pi * ((rows[:, None] * cols[None, :]) % period).astype(np.float64) / period
        return np.cos(ang), np.sin(ang)
    q = n_pos // POS_RADIX
    ch = np.arange(FGROUP_DIM, dtype=np.int64)
    cc, sc = cos_sin(ch, ch, FGROUP_DIM)
    n2 = np.arange(q, dtype=np.int64)
    cq, sq = cos_sin(n2, n2, q)
    tc, ts = cos_sin(np.arange(POS_RADIX, dtype=np.int64), n2, n_pos)
    lanes = np.ones((1, F_W))
    chan = np.concatenate([cc, sc], axis=1).astype(np.float32)
    pos = np.concatenate([cq, sq], axis=1).astype(np.float32)
    twc = (tc.reshape(n_pos, 1) * lanes).astype(np.float32)
    tws = (ts.reshape(n_pos, 1) * lanes).astype(np.float32)
    return chan, pos, twc, tws


def _fourier_kernel(f_ref, chan_ref, pos_ref, twc_ref, tws_ref, o_ref, ur_ref, ui_ref, t_ref, y_ref, *, scale):
    assert POS_RADIX == 4
    n_pos = f_ref.shape[1]
    q = n_pos // POS_RADIX
    for g in range(N_FGROUPS):
        cols = slice(g * FGROUP_DIM, (g + 1) * FGROUP_DIM)
        xcs = jnp.dot(f_ref[0, :, cols], chan_ref[...], preferred_element_type=F32)
        ur_ref[:, cols] = xcs[:, :FGROUP_DIM]
        ui_ref[:, cols] = -xcs[:, FGROUP_DIM:]
    ur = [ur_ref[i * q:(i + 1) * q, :] for i in range(POS_RADIX)]
    ui = [ui_ref[i * q:(i + 1) * q, :] for i in range(POS_RADIX)]
    er, ei = ur[0] + ur[2], ui[0] + ui[2]
    sr, si = ur[1] + ur[3], ui[1] + ui[3]
    dr, di = ur[0] - ur[2], ui[0] - ui[2]
    gr, gi = ur[1] - ur[3], ui[1] - ui[3]
    butterflies = [(er + sr, ei + si), (dr + gi, di - gr), (er - sr, ei - si), (dr - gi, di + gr)]
    for k1, (tr, ti) in enumerate(butterflies):
        if k1:
            c = twc_ref[k1 * q:(k1 + 1) * q, :]
            s = tws_ref[k1 * q:(k1 + 1) * q, :]
            tr, ti = tr * c + ti * s, ti * c - tr * s
        t_ref[k1, 0:q, :] = tr.astype(BF16)
        t_ref[k1, q:2 * q, :] = ti.astype(BF16)
    for k1 in range(POS_RADIX):
        y = jnp.dot(pos_ref[...], t_ref[k1], preferred_element_type=F32) * scale
        for g in range(N_FGROUPS):
            y_ref[g, pl.ds(k1, q, stride=POS_RADIX), :] = y[:, g * FGROUP_DIM:(g + 1) * FGROUP_DIM]
    o_ref[0] = jnp.concatenate([y_ref[g] for g in range(N_FGROUPS)], axis=1).astype(BF16)


def _fourier(f):
    B, S, W = f.shape
    assert S % POS_RADIX == 0
    q = S // POS_RADIX
    chan, pos, twc, tws = _dft_tables(S)
    scale = float(1.0 / np.sqrt(S * FGROUP_DIM))
    return pl.pallas_call(
        functools.partial(_fourier_kernel, scale=scale),
        grid=(B,),
        in_specs=[pl.BlockSpec((1, S, W), lambda b: (b, 0, 0)),
                  _const_spec((FGROUP_DIM, 2 * FGROUP_DIM)),
                  _const_spec((q, 2 * q)),
                  _const_spec((S, W)), _const_spec((S, W))],
        out_specs=pl.BlockSpec((1, S, W), lambda b: (b, 0, 0)),
        out_shape=jax.ShapeDtypeStruct((B, S, W), BF16),
        scratch_shapes=[pltpu.VMEM((S, W), F32), pltpu.VMEM((S, W), F32),
                        pltpu.VMEM((POS_RADIX, 2 * q, W), BF16),
                        pltpu.VMEM((N_FGROUPS, S, FGROUP_DIM), F32)],
        compiler_params=_params("parallel"),
        name="fourier",
    )(f, jnp.asarray(chan).astype(BF16), jnp.asarray(pos).astype(BF16), jnp.asarray(twc), jnp.asarray(tws))


def _merge_kernel(x_ref, shift_ref, scale_ref, gate_ref, gnorm_ref, wg_ref, bg_ref, attn_ref, four_ref,
                  wap_ref, wfp_ref, wo_ref, o_ref):
    x = x_ref[0]
    h = _modulated_norm(x, gnorm_ref[...], shift_ref[0], scale_ref[0]).astype(BF16)
    gates = _sigmoid(jnp.dot(h, wg_ref[...], preferred_element_type=F32) + bg_ref[...])
    a = jnp.dot(attn_ref[0], wap_ref[...], preferred_element_type=F32)
    fo = jnp.dot(four_ref[0], wfp_ref[...], preferred_element_type=F32)
    merged = gates[:, :D_MODEL] * a + gates[:, D_MODEL:] * fo
    mix = jnp.dot(merged.astype(BF16), wo_ref[...], preferred_element_type=F32)
    o_ref[0] = x + gate_ref[0] * mix


def _merge(x, mods3, gnorm, wg, bg, attn, four, wap, wfp, wo):
    B, S, D = x.shape
    tm = TM_MERGE
    return pl.pallas_call(
        _merge_kernel,
        grid=(B, S // tm),
        in_specs=[pl.BlockSpec((1, tm, D), lambda b, i: (b, i, 0)), _mod_spec(0), _mod_spec(1), _mod_spec(2),
                  _const_spec((1, D)),
                  _const_spec((D, N_BRANCH * D)), _const_spec((1, N_BRANCH * D)),
                  pl.BlockSpec((1, tm, Q_W), lambda b, i: (b, i, 0)),
                  pl.BlockSpec((1, tm, F_W), lambda b, i: (b, i, 0)),
                  _const_spec((Q_W, D)), _const_spec((F_W, D)), _const_spec((D, D))],
        out_specs=pl.BlockSpec((1, tm, D), lambda b, i: (b, i, 0)),
        out_shape=jax.ShapeDtypeStruct((B, S, D), F32),
        compiler_params=_params("parallel", "parallel"),
        name="merge",
    )(x, mods3, mods3, mods3, gnorm, wg, bg, attn, four, wap, wfp, wo)


def _ffn_kernel(x_ref, shift_ref, scale_ref, gate_ref, gnorm_ref, wgu_ref, wdown_ref, gfinal_ref, o_ref):
    x = x_ref[0]
    d_ff = wdown_ref.shape[0]
    h = _modulated_norm(x, gnorm_ref[...], shift_ref[0], scale_ref[0]).astype(BF16)
    ffn = None
    for lo in range(0, d_ff, FF_CHUNK):
        gate = jnp.dot(h, wgu_ref[:, lo:lo + FF_CHUNK], preferred_element_type=F32)
        up = jnp.dot(h, wgu_ref[:, d_ff + lo:d_ff + lo + FF_CHUNK], preferred_element_type=F32)
        act = (gate * _sigmoid(gate) * up).astype(BF16)
        part = jnp.dot(act, wdown_ref[lo:lo + FF_CHUNK, :], preferred_element_type=F32)
        ffn = part if ffn is None else ffn + part
    x2 = x + gate_ref[0] * ffn
    ms = jnp.mean(x2 * x2, axis=-1, keepdims=True)
    o_ref[0] = x2 * lax.rsqrt(ms + EPS) * gfinal_ref[...]


def _ffn(x, mods3, gnorm, wgu, wdown, gfinal):
    B, S, D = x.shape
    d_ff = wdown.shape[0]
    tm = TM_FFN
    return pl.pallas_call(
        _ffn_kernel,
        grid=(B, S // tm),
        in_specs=[pl.BlockSpec((1, tm, D), lambda b, i: (b, i, 0)), _mod_spec(3), _mod_spec(4), _mod_spec(5),
                  _const_spec((1, D)),
                  _const_spec((D, 2 * d_ff)), _const_spec((d_ff, D)), _const_spec((1, D))],
        out_specs=pl.BlockSpec((1, tm, D), lambda b, i: (b, i, 0)),
        out_shape=jax.ShapeDtypeStruct((B, S, D), F32),
        compiler_params=_params("parallel", "parallel"),
        name="ffn",
    )(x, mods3, mods3, mods3, gnorm, wgu, wdown, gfinal)


@functools.lru_cache(maxsize=None)
def _rope_tables_t(seq):
    rows = seq // GRID_W
    t_row = np.repeat(np.arange(rows, dtype=np.float64), GRID_W)
    t_col = np.tile(np.arange(GRID_W, dtype=np.float64), rows)
    inv_freq = ROPE_THETA ** (-np.arange(0, ROPE_AXIS_DIM, 2, dtype=np.float64) / ROPE_AXIS_DIM)
    ang = np.concatenate([t_row[:, None] * inv_freq, t_col[:, None] * inv_freq], axis=-1)
    return np.cos(ang).T.astype(np.float32), np.sin(ang).T.astype(np.float32)


def kernel(x, c, ctx, c_ctx, w_ada, b_ada, g_norm_mix, g_norm_ffn, w_in, b_gate, g_q, g_k, w_attn_proj,
           w_fourier_proj, w_o, w_gate_up, w_down, g_final):
    B, S, D = x.shape
    C = ctx.shape[1]
    assert D == D_MODEL and w_in.shape[0] == 1 and B < COND_ROWS
    assert S % TM_PROJ == 0 and TM_PROJ % TM_SUB == 0 and S % TM_MERGE == 0 and S % TM_FFN == 0 and S % TQ == 0
    assert w_down.shape[1] % FF_CHUNK == 0
    layer = 0

    cond = jnp.zeros((COND_ROWS, D), F32).at[:B].set(c).at[B].set(c_ctx)
    mods = _adaln(cond, w_ada[layer], b_ada[layer])
    mods3 = mods.reshape(COND_ROWS * N_MODS, 1, D)

    w_in_l = w_in[layer]
    wqkvT = w_in_l[:, :V_END].T.astype(BF16)
    wkvT = w_in_l[:, Q_W:V_END].T.astype(BF16)
    wf = w_in_l[:, V_END:F_END].astype(BF16)
    wg = w_in_l[:, F_END:].astype(BF16)
    gnorm_mix = g_norm_mix[layer].reshape(1, D)
    gnorm_ffn = g_norm_ffn[layer].reshape(1, D)

    gq_col = (g_q[layer] * (HEAD_DIM ** -0.5 * LOG2_E)).reshape(HEAD_DIM, 1)
    gk_col = g_k[layer].reshape(HEAD_DIM, 1)
    gq_b = jnp.broadcast_to(gq_col, (HEAD_DIM, TM_SUB))
    gk_b = jnp.broadcast_to(gk_col, (HEAD_DIM, TM_SUB))
    gk_ctx = jnp.broadcast_to(gk_col, (HEAD_DIM, C))
    cosT, sinT = (jnp.asarray(t) for t in _rope_tables_t(S))

    qT, k_all, vT_all, f = _in_proj(x, mods3, gnorm_mix, wqkvT, wf, gq_b, gk_b, cosT, sinT, C)
    k_all, vT_all = _ctx_kv(ctx, mods3, gnorm_mix, wkvT, gk_ctx, k_all, vT_all)
    attn = _attention(qT, k_all, vT_all)
    four = _fourier(f)
    x1 = _merge(x, mods3, gnorm_mix, wg, b_gate[layer].reshape(1, -1), attn, four,
                w_attn_proj[layer].astype(BF16), w_fourier_proj[layer].astype(BF16), w_o[layer].astype(BF16))
    return _ffn(x1, mods3, gnorm_ffn, w_gate_up[layer].astype(BF16), w_down[layer].astype(BF16),
                g_final.reshape(1, D))
```

```python
import functools

import numpy as np
import jax
import jax.numpy as jnp
from jax import lax
from jax.experimental import pallas as pl
from jax.experimental.pallas import tpu as pltpu

D_MODEL = 1024
GRID_W = 64
HEAD_DIM = 64
N_HEADS = 16
N_KV_HEADS = 4
GROUP = N_HEADS // N_KV_HEADS
ROPE_AXIS_DIM = HEAD_DIM // 2
ROPE_THETA = 10000.0
N_FGROUPS = 4
FGROUP_DIM = 128
N_BRANCH = 2
Q_W = N_HEADS * HEAD_DIM
KV_W = N_KV_HEADS * HEAD_DIM
F_W = N_FGROUPS * FGROUP_DIM
K_END = Q_W + KV_W
V_END = K_END + KV_W
F_END = V_END + F_W
EPS = 1e-6
LOG2_E = 1.4426950408889634

BF16 = jnp.bfloat16
F32 = jnp.float32

VMEM_LIMIT_BYTES = 56 * 1024 * 1024

COND_ROWS = 16
N_MODS = 6
ADALN_COLS = 1536
CTX_BATCH = 4
TM_PROJ = 2048
TM_SUB = 512
TM_MERGE = 1024
TM_FFN = 1024
FF_CHUNK = 256
TQ = 512
KEY_CHUNK = 256


def _params(*sem):
    return pltpu.CompilerParams(dimension_semantics=sem, vmem_limit_bytes=VMEM_LIMIT_BYTES)


def _const_spec(shape):
    nd = len(shape)
    return pl.BlockSpec(shape, lambda *_: (0,) * nd, pipeline_mode=pl.Buffered(1))


def _mod_spec(k, row=None):
    if row is None:
        return pl.BlockSpec((1, 1, D_MODEL), lambda b, *_: (b * N_MODS + k, 0, 0))
    return pl.BlockSpec((1, 1, D_MODEL), lambda *_: (row * N_MODS + k, 0, 0))


def _sigmoid(x):
    return 1.0 / (1.0 + jnp.exp(-x))


def _modulated_norm(x, gain, shift, scale):
    ms = jnp.mean(x * x, axis=-1, keepdims=True)
    y = x * lax.rsqrt(ms + EPS) * gain
    return y * (1.0 + scale) + shift


def _head_norm_t(zh, gain):
    ms = jnp.mean(zh * zh, axis=0, keepdims=True)
    return zh * lax.rsqrt(ms + EPS) * gain


def _rope_t(zn, cos, sin):
    half = HEAD_DIM // 2
    x1, x2 = zn[:half], zn[half:]
    return x1 * cos - x2 * sin, x2 * cos + x1 * sin


def _adaln_kernel(cond_ref, w_ref, b_ref, o_ref):
    c = cond_ref[...]
    s = (c * _sigmoid(c)).astype(BF16)
    o_ref[...] = jnp.dot(s, w_ref[...].astype(BF16), preferred_element_type=F32) + b_ref[...]


def _adaln(cond, w, b):
    n = w.shape[1]
    tn = ADALN_COLS
    assert n % tn == 0
    return pl.pallas_call(
        _adaln_kernel,
        grid=(n // tn,),
        in_specs=[_const_spec((COND_ROWS, D_MODEL)),
                  pl.BlockSpec((D_MODEL, tn), lambda j: (0, j)),
                  pl.BlockSpec((1, tn), lambda j: (0, j))],
        out_specs=pl.BlockSpec((COND_ROWS, tn), lambda j: (0, j)),
        out_shape=jax.ShapeDtypeStruct((COND_ROWS, n), F32),
        compiler_params=_params("parallel"),
        name="adaln",
    )(cond, w, b.reshape(1, n))


def _in_proj_kernel(x_ref, shift_ref, scale_ref, gnorm_ref, wqkvT_ref, wf_ref, gq_ref, gk_ref,
                    cos_ref, sin_ref, qT_ref, k_ref, vT_ref, f_ref):
    tm = x_ref.shape[1]
    gq, gk = gq_ref[...], gk_ref[...]
    half = HEAD_DIM // 2
    for j in range(tm // TM_SUB):
        tok = slice(j * TM_SUB, (j + 1) * TM_SUB)
        h = _modulated_norm(x_ref[0, tok, :], gnorm_ref[...], shift_ref[0], scale_ref[0]).astype(BF16)
        zT = lax.dot_general(wqkvT_ref[...], h, (((1,), (1,)), ((), ())), preferred_element_type=F32)
        cos, sin = cos_ref[:, tok], sin_ref[:, tok]
        for i in range(N_HEADS):
            lo = i * HEAD_DIM
            a, b = _rope_t(_head_norm_t(zT[lo:lo + HEAD_DIM], gq), cos, sin)
            qT_ref[0, lo:lo + half, tok] = a.astype(BF16)
            qT_ref[0, lo + half:lo + HEAD_DIM, tok] = b.astype(BF16)
        k_parts = []
        for i in range(N_KV_HEADS):
            lo = Q_W + i * HEAD_DIM
            k_parts.extend(_rope_t(_head_norm_t(zT[lo:lo + HEAD_DIM], gk), cos, sin))
        k_tok = jnp.concatenate(k_parts, axis=0).T
        for g in range(N_KV_HEADS):
            k_ref[0, g, tok, :] = k_tok[:, g * HEAD_DIM:(g + 1) * HEAD_DIM].astype(BF16)
        vT_ref[0, :, tok] = zT[K_END:V_END].astype(BF16)
        f_ref[0, tok, :] = jnp.dot(h, wf_ref[...], preferred_element_type=F32).astype(BF16)


def _in_proj(x, mods3, gnorm, wqkvT, wf, gq_b, gk_b, cosT, sinT, n_ctx):
    B, S, D = x.shape
    tm = TM_PROJ
    half = HEAD_DIM // 2
    return pl.pallas_call(
        _in_proj_kernel,
        grid=(B, S // tm),
        in_specs=[pl.BlockSpec((1, tm, D), lambda b, i: (b, i, 0)),
                  _mod_spec(0), _mod_spec(1),
                  _const_spec((1, D)),
                  _const_spec((V_END, D)), _const_spec((D, F_W)),
                  _const_spec((HEAD_DIM, TM_SUB)), _const_spec((HEAD_DIM, TM_SUB)),
                  pl.BlockSpec((half, tm), lambda b, i: (0, i)),
                  pl.BlockSpec((half, tm), lambda b, i: (0, i))],
        out_specs=[pl.BlockSpec((1, Q_W, tm), lambda b, i: (b, 0, i)),
                   pl.BlockSpec((1, N_KV_HEADS, tm, HEAD_DIM), lambda b, i: (b, 0, i, 0)),
                   pl.BlockSpec((1, KV_W, tm), lambda b, i: (b, 0, i)),
                   pl.BlockSpec((1, tm, F_W), lambda b, i: (b, i, 0))],
        out_shape=[jax.ShapeDtypeStruct((B, Q_W, S), BF16),
                   jax.ShapeDtypeStruct((B, N_KV_HEADS, S + n_ctx, HEAD_DIM), BF16),
                   jax.ShapeDtypeStruct((B, KV_W, S + n_ctx), BF16),
                   jax.ShapeDtypeStruct((B, S, F_W), BF16)],
        compiler_params=_params("parallel", "parallel"),
        name="in_proj",
    )(x, mods3, mods3, gnorm, wqkvT, wf, gq_b, gk_b, cosT, sinT)


def _ctx_kv_kernel(x_ref, shift_ref, scale_ref, gnorm_ref, wkvT_ref, gk_ref, k_in_ref, vT_in_ref,
                   k_ref, vT_ref):
    del k_in_ref, vT_in_ref
    gk = gk_ref[...]
    for i in range(x_ref.shape[0]):
        h = _modulated_norm(x_ref[i], gnorm_ref[...], shift_ref[0], scale_ref[0]).astype(BF16)
        zT = lax.dot_general(wkvT_ref[...], h, (((1,), (1,)), ((), ())), preferred_element_type=F32)
        kT = jnp.concatenate(
            [_head_norm_t(zT[j * HEAD_DIM:(j + 1) * HEAD_DIM], gk) for j in range(N_KV_HEADS)], axis=0)
        k_tok = kT.T
        for g in range(N_KV_HEADS):
            k_ref[i, g] = k_tok[:, g * HEAD_DIM:(g + 1) * HEAD_DIM].astype(BF16)
        vT_ref[i] = zT[KV_W:].astype(BF16)


def _ctx_kv(ctx, mods3, gnorm, wkvT, gk_b, k_all, vT_all):
    B, C, D = ctx.shape
    n_lat = k_all.shape[2] - C
    assert n_lat % C == 0 and B % CTX_BATCH == 0
    blk = n_lat // C
    nb = CTX_BATCH
    return pl.pallas_call(
        _ctx_kv_kernel,
        grid=(B // nb,),
        in_specs=[pl.BlockSpec((nb, C, D), lambda b: (b, 0, 0)),
                  _mod_spec(0, row=B), _mod_spec(1, row=B), _const_spec((1, D)),
                  _const_spec((2 * KV_W, D)), _const_spec((HEAD_DIM, C)),
                  pl.BlockSpec(memory_space=pl.ANY), pl.BlockSpec(memory_space=pl.ANY)],
        out_specs=[pl.BlockSpec((nb, N_KV_HEADS, C, HEAD_DIM), lambda b: (b, 0, blk, 0)),
                   pl.BlockSpec((nb, KV_W, C), lambda b: (b, 0, blk))],
        out_shape=[jax.ShapeDtypeStruct(k_all.shape, BF16),
                   jax.ShapeDtypeStruct(vT_all.shape, BF16)],
        input_output_aliases={6: 0, 7: 1},
        compiler_params=_params("parallel"),
        name="ctx_kv",
    )(ctx, mods3, mods3, gnorm, wkvT, gk_b, k_all, vT_all)


def _attention_kernel(qT_ref, qn_ref, k_ref, kn_ref, vT_ref, o_ref, vc_ref, s0_ref, s1_ref, m0_ref, m1_ref, oT_ref):
    first = jnp.logical_and(pl.program_id(0) == 0, pl.program_id(1) == 0)
    tq = qT_ref.shape[2]
    n_chunks = k_ref.shape[2] // KEY_CHUNK

    @pl.when(pl.program_id(1) == 0)
    def _():
        for g in range(N_KV_HEADS):
            for c in range(n_chunks):
                vc_ref[g, c] = vT_ref[0, g * HEAD_DIM:(g + 1) * HEAD_DIM, c * KEY_CHUNK:(c + 1) * KEY_CHUNK]

    def head_rows(t):
        return slice(t * HEAD_DIM, (t + 1) * HEAD_DIM)

    def step(t_b, sb, sa, a_keys=None, a_q=None):
        m8 = jnp.full((8, tq), -jnp.inf, F32)
        l8 = jnp.zeros((8, tq), F32)
        o = None
        for c in range(n_chunks):
            rows = slice(c * KEY_CHUNK, (c + 1) * KEY_CHUNK)
            srows = pl.ds(pl.multiple_of(row0 + c * KEY_CHUNK, KEY_CHUNK), KEY_CHUNK)
            if sb is not None:
                e = jnp.exp2(sb[0][srows, :] - sb[1][0:1, :])
                l8 = l8 + jnp.sum(e.reshape(KEY_CHUNK // 8, 8, tq), axis=0)
                d = jnp.dot(vc_ref[t_b // GROUP, c], e.astype(BF16), preferred_element_type=F32)
                o = d if o is None else o + d
            if sa is not None:
                s = jnp.dot(a_keys(rows), a_q(), preferred_element_type=F32)
                sa[0][srows, :] = s
                m8 = jnp.maximum(m8, jnp.max(s.reshape(KEY_CHUNK // 8, 8, tq), axis=0))
        if sa is not None:
            sa[1][...] = jnp.broadcast_to(jnp.max(m8, axis=0, keepdims=True), (8, tq))
        if sb is not None:
            oT_ref[head_rows(t_b), :] = o / jnp.sum(l8, axis=0, keepdims=True)

    def cur(t):
        return (lambda rows: k_ref[0, t // GROUP, rows, :]), (lambda: qT_ref[0, head_rows(t), :])

    bufs = ((s0_ref, m0_ref), (s1_ref, m1_ref))
    row0 = jnp.minimum(pl.program_id(0), 0)

    @pl.when(first)
    def _():
        step(-1, None, bufs[0], *cur(0))

    for t in range(N_HEADS - 1):
        step(t, bufs[t % 2], bufs[(t + 1) % 2], *cur(t + 1))
    step(N_HEADS - 1, bufs[(N_HEADS - 1) % 2], bufs[N_HEADS % 2],
         lambda rows: kn_ref[0, 0, rows, :], lambda: qn_ref[0])
    o_ref[0] = oT_ref[...].T.astype(BF16)


def _attention(qT, k_all, vT_all):
    B, _, S = qT.shape
    T = k_all.shape[2]
    assert T % KEY_CHUNK == 0 and N_HEADS % 2 == 0
    n_chunks = T // KEY_CHUNK
    nq = S // TQ

    def nxt(b, i):
        wrap = i + 1 == nq
        return jnp.where(wrap, jnp.minimum(b + 1, B - 1), b), jnp.where(wrap, 0, i + 1)

    return pl.pallas_call(
        _attention_kernel,
        grid=(B, nq),
        in_specs=[pl.BlockSpec((1, Q_W, TQ), lambda b, i: (b, 0, i)),
                  pl.BlockSpec((1, HEAD_DIM, TQ), lambda b, i: (nxt(b, i)[0], 0, nxt(b, i)[1])),
                  pl.BlockSpec((1, N_KV_HEADS, T, HEAD_DIM), lambda b, i: (b, 0, 0, 0)),
                  pl.BlockSpec((1, 1, T, HEAD_DIM), lambda b, i: (nxt(b, i)[0], 0, 0, 0)),
                  pl.BlockSpec((1, KV_W, T), lambda b, i: (b, 0, 0))],
        out_specs=pl.BlockSpec((1, TQ, Q_W), lambda b, i: (b, i, 0)),
        out_shape=jax.ShapeDtypeStruct((B, S, Q_W), BF16),
        scratch_shapes=[pltpu.VMEM((N_KV_HEADS, n_chunks, HEAD_DIM, KEY_CHUNK), BF16),
                        pltpu.VMEM((T, TQ), F32), pltpu.VMEM((T, TQ), F32),
                        pltpu.VMEM((8, TQ), F32), pltpu.VMEM((8, TQ), F32),
                        pltpu.VMEM((Q_W, TQ), F32)],
        compiler_params=_params("arbitrary", "arbitrary"),
        name="attention",
    )(qT, qT, k_all, k_all, vT_all)


POS_RADIX = 4


@functools.lru_cache(maxsize=None)
def _dft_tables(n_pos):
    def cos_sin(rows, cols, period):
        ang = 2.0 * np.pi * ((rows[:, None] * cols[None, :]) % period).astype(np.float64) / period
        return np.cos(ang), np.sin(ang)
    q = n_pos // POS_RADIX
    ch = np.arange(FGROUP_DIM, dtype=np.int64)
    cc, sc = cos_sin(ch, ch, FGROUP_DIM)
    n2 = np.arange(q, dtype=np.int64)
    cq, sq = cos_sin(n2, n2, q)
    tc, ts = cos_sin(np.arange(POS_RADIX, dtype=np.int64), n2, n_pos)
    lanes = np.ones((1, F_W))
    chan = np.concatenate([cc, sc], axis=1).astype(np.float32)
    pos = np.concatenate([cq, sq], axis=1).astype(np.float32)
    twc = (tc.reshape(n_pos, 1) * lanes).astype(np.float32)
    tws = (ts.reshape(n_pos, 1) * lanes).astype(np.float32)
    return chan, pos, twc, tws


def _fourier_kernel(f_ref, chan_ref, pos_ref, twc_ref, tws_ref, o_ref, ur_ref, ui_ref, t_ref, y_ref, *, scale):
    assert POS_RADIX == 4
    n_pos = f_ref.shape[1]
    q = n_pos // POS_RADIX
    for g in range(N_FGROUPS):
        cols = slice(g * FGROUP_DIM, (g + 1) * FGROUP_DIM)
        xcs = jnp.dot(f_ref[0, :, cols], chan_ref[...], preferred_element_type=F32)
        ur_ref[:, cols] = xcs[:, :FGROUP_DIM]
        ui_ref[:, cols] = -xcs[:, FGROUP_DIM:]
    ur = [ur_ref[i * q:(i + 1) * q, :] for i in range(POS_RADIX)]
    ui = [ui_ref[i * q:(i + 1) * q, :] for i in range(POS_RADIX)]
    er, ei = ur[0] + ur[2], ui[0] + ui[2]
    sr, si = ur[1] + ur[3], ui[1] + ui[3]
    dr, di = ur[0] - ur[2], ui[0] - ui[2]
    gr, gi = ur[1] - ur[3], ui[1] - ui[3]
    butterflies = [(er + sr, ei + si), (dr + gi, di - gr), (er - sr, ei - si), (dr - gi, di + gr)]
    for k1, (tr, ti) in enumerate(butterflies):
        if k1:
            c = twc_ref[k1 * q:(k1 + 1) * q, :]
            s = tws_ref[k1 * q:(k1 + 1) * q, :]
            tr, ti = tr * c + ti * s, ti * c - tr * s
        t_ref[k1, 0:q, :] = tr.astype(BF16)
        t_ref[k1, q:2 * q, :] = ti.astype(BF16)
    for k1 in range(POS_RADIX):
        y = jnp.dot(pos_ref[...], t_ref[k1], preferred_element_type=F32) * scale
        for g in range(N_FGROUPS):
            y_ref[g, pl.ds(k1, q, stride=POS_RADIX), :] = y[:, g * FGROUP_DIM:(g + 1) * FGROUP_DIM]
    o_ref[0] = jnp.concatenate([y_ref[g] for g in range(N_FGROUPS)], axis=1).astype(BF16)


def _fourier(f):
    B, S, W = f.shape
    assert S % POS_RADIX == 0
    q = S // POS_RADIX
    chan, pos, twc, tws = _dft_tables(S)
    scale = float(1.0 / np.sqrt(S * FGROUP_DIM))
    return pl.pallas_call(
        functools.partial(_fourier_kernel, scale=scale),
        grid=(B,),
        in_specs=[pl.BlockSpec((1, S, W), lambda b: (b, 0, 0)),
                  _const_spec((FGROUP_DIM, 2 * FGROUP_DIM)),
                  _const_spec((q, 2 * q)),
                  _const_spec((S, W)), _const_spec((S, W))],
        out_specs=pl.BlockSpec((1, S, W), lambda b: (b, 0, 0)),
        out_shape=jax.ShapeDtypeStruct((B, S, W), BF16),
        scratch_shapes=[pltpu.VMEM((S, W), F32), pltpu.VMEM((S, W), F32),
                        pltpu.VMEM((POS_RADIX, 2 * q, W), BF16),
                        pltpu.VMEM((N_FGROUPS, S, FGROUP_DIM), F32)],
        compiler_params=_params("parallel"),
        name="fourier",
    )(f, jnp.asarray(chan).astype(BF16), jnp.asarray(pos).astype(BF16), jnp.asarray(twc), jnp.asarray(tws))


def _merge_kernel(x_ref, shift_ref, scale_ref, gate_ref, gnorm_ref, wg_ref, bg_ref, attn_ref, four_ref,
                  wap_ref, wfp_ref, wo_ref, o_ref):
    x = x_ref[0]
    h = _modulated_norm(x, gnorm_ref[...], shift_ref[0], scale_ref[0]).astype(BF16)
    gates = _sigmoid(jnp.dot(h, wg_ref[...], preferred_element_type=F32) + bg_ref[...])
    a = jnp.dot(attn_ref[0], wap_ref[...], preferred_element_type=F32)
    fo = jnp.dot(four_ref[0], wfp_ref[...], preferred_element_type=F32)
    merged = gates[:, :D_MODEL] * a + gates[:, D_MODEL:] * fo
    mix = jnp.dot(merged.astype(BF16), wo_ref[...], preferred_element_type=F32)
    o_ref[0] = x + gate_ref[0] * mix


def _merge(x, mods3, gnorm, wg, bg, attn, four, wap, wfp, wo):
    B, S, D = x.shape
    tm = TM_MERGE
    return pl.pallas_call(
        _merge_kernel,
        grid=(B, S // tm),
        in_specs=[pl.BlockSpec((1, tm, D), lambda b, i: (b, i, 0)), _mod_spec(0), _mod_spec(1), _mod_spec(2),
                  _const_spec((1, D)),
                  _const_spec((D, N_BRANCH * D)), _const_spec((1, N_BRANCH * D)),
                  pl.BlockSpec((1, tm, Q_W), lambda b, i: (b, i, 0)),
                  pl.BlockSpec((1, tm, F_W), lambda b, i: (b, i, 0)),
                  _const_spec((Q_W, D)), _const_spec((F_W, D)), _const_spec((D, D))],
        out_specs=pl.BlockSpec((1, tm, D), lambda b, i: (b, i, 0)),
        out_shape=jax.ShapeDtypeStruct((B, S, D), F32),
        compiler_params=_params("parallel", "parallel"),
        name="merge",
    )(x, mods3, mods3, mods3, gnorm, wg, bg, attn, four, wap, wfp, wo)


def _ffn_kernel(x_ref, shift_ref, scale_ref, gate_ref, gnorm_ref, wgu_ref, wdown_ref, gfinal_ref, o_ref):
    x = x_ref[0]
    d_ff = wdown_ref.shape[0]
    h = _modulated_norm(x, gnorm_ref[...], shift_ref[0], scale_ref[0]).astype(BF16)
    ffn = None
    for lo in range(0, d_ff, FF_CHUNK):
        gate = jnp.dot(h, wgu_ref[:, lo:lo + FF_CHUNK], preferred_element_type=F32)
        up = jnp.dot(h, wgu_ref[:, d_ff + lo:d_ff + lo + FF_CHUNK], preferred_element_type=F32)
        act = (gate * _sigmoid(gate) * up).astype(BF16)
        part = jnp.dot(act, wdown_ref[lo:lo + FF_CHUNK, :], preferred_element_type=F32)
        ffn = part if ffn is None else ffn + part
    x2 = x + gate_ref[0] * ffn
    ms = jnp.mean(x2 * x2, axis=-1, keepdims=True)
    o_ref[0] = x2 * lax.rsqrt(ms + EPS) * gfinal_ref[...]


def _ffn(x, mods3, gnorm, wgu, wdown, gfinal):
    B, S, D = x.shape
    d_ff = wdown.shape[0]
    tm = TM_FFN
    return pl.pallas_call(
        _ffn_kernel,
        grid=(B, S // tm),
        in_specs=[pl.BlockSpec((1, tm, D), lambda b, i: (b, i, 0)), _mod_spec(3), _mod_spec(4), _mod_spec(5),
                  _const_spec((1, D)),
                  _const_spec((D, 2 * d_ff)), _const_spec((d_ff, D)), _const_spec((1, D))],
        out_specs=pl.BlockSpec((1, tm, D), lambda b, i: (b, i, 0)),
        out_shape=jax.ShapeDtypeStruct((B, S, D), F32),
        compiler_params=_params("parallel", "parallel"),
        name="ffn",
    )(x, mods3, mods3, mods3, gnorm, wgu, wdown, gfinal)


@functools.lru_cache(maxsize=None)
def _rope_tables_t(seq):
    rows = seq // GRID_W
    t_row = np.repeat(np.arange(rows, dtype=np.float64), GRID_W)
    t_col = np.tile(np.arange(GRID_W, dtype=np.float64), rows)
    inv_freq = ROPE_THETA ** (-np.arange(0, ROPE_AXIS_DIM, 2, dtype=np.float64) / ROPE_AXIS_DIM)
    ang = np.concatenate([t_row[:, None] * inv_freq, t_col[:, None] * inv_freq], axis=-1)
    return np.cos(ang).T.astype(np.float32), np.sin(ang).T.astype(np.float32)


def kernel(x, c, ctx, c_ctx, w_ada, b_ada, g_norm_mix, g_norm_ffn, w_in, b_gate, g_q, g_k, w_attn_proj,
           w_fourier_proj, w_o, w_gate_up, w_down, g_final):
    B, S, D = x.shape
    C = ctx.shape[1]
    assert D == D_MODEL and w_in.shape[0] == 1 and B < COND_ROWS
    assert S % TM_PROJ == 0 and TM_PROJ % TM_SUB == 0 and S % TM_MERGE == 0 and S % TM_FFN == 0 and S % TQ == 0
    assert w_down.shape[1] % FF_CHUNK == 0
    layer = 0

    cond = jnp.zeros((COND_ROWS, D), F32).at[:B].set(c).at[B].set(c_ctx)
    mods = _adaln(cond, w_ada[layer], b_ada[layer])
    mods3 = mods.reshape(COND_ROWS * N_MODS, 1, D)

    w_in_l = w_in[layer]
    wqkvT = w_in_l[:, :V_END].T.astype(BF16)
    wkvT = w_in_l[:, Q_W:V_END].T.astype(BF16)
    wf = w_in_l[:, V_END:F_END].astype(BF16)
    wg = w_in_l[:, F_END:].astype(BF16)
    gnorm_mix = g_norm_mix[layer].reshape(1, D)
    gnorm_ffn = g_norm_ffn[layer].reshape(1, D)

    gq_col = (g_q[layer] * (HEAD_DIM ** -0.5 * LOG2_E)).reshape(HEAD_DIM, 1)
    gk_col = g_k[layer].reshape(HEAD_DIM, 1)
    gq_b = jnp.broadcast_to(gq_col, (HEAD_DIM, TM_SUB))
    gk_b = jnp.broadcast_to(gk_col, (HEAD_DIM, TM_SUB))
    gk_ctx = jnp.broadcast_to(gk_col, (HEAD_DIM, C))
    cosT, sinT = (jnp.asarray(t) for t in _rope_tables_t(S))

    qT, k_all, vT_all, f = _in_proj(x, mods3, gnorm_mix, wqkvT, wf, gq_b, gk_b, cosT, sinT, C)
    k_all, vT_all = _ctx_kv(ctx, mods3, gnorm_mix, wkvT, gk_ctx, k_all, vT_all)
    attn = _attention(qT, k_all, vT_all)
    four = _fourier(f)
    x1 = _merge(x, mods3, gnorm_mix, wg, b_gate[layer].reshape(1, -1), attn, four,
                w_attn_proj[layer].astype(BF16), w_fourier_proj[layer].astype(BF16), w_o[layer].astype(BF16))
    return _ffn(x1, mods3, gnorm_ffn, w_gate_up[layer].astype(BF16), w_down[layer].astype(BF16),
                g_final.reshape(1, D))
```

```python
import functools

import numpy as np
import jax
import jax.numpy as jnp
from jax import lax
from jax.experimental import pallas as pl
from jax.experimental.pallas import tpu as pltpu

D_MODEL = 1024
GRID_W = 64
HEAD_DIM = 64
N_HEADS = 16
N_KV_HEADS = 4
GROUP = N_HEADS // N_KV_HEADS
ROPE_AXIS_DIM = HEAD_DIM // 2
ROPE_THETA = 10000.0
N_FGROUPS = 4
FGROUP_DIM = 128
N_BRANCH = 2
Q_W = N_HEADS * HEAD_DIM
KV_W = N_KV_HEADS * HEAD_DIM
F_W = N_FGROUPS * FGROUP_DIM
K_END = Q_W + KV_W
V_END = K_END + KV_W
F_END = V_END + F_W
EPS = 1e-6
LOG2_E = 1.4426950408889634

BF16 = jnp.bfloat16
F32 = jnp.float32

VMEM_LIMIT_BYTES = 56 * 1024 * 1024

COND_ROWS = 16
N_MODS = 6
ADALN_COLS = 1536
CTX_BATCH = 1
TM_PROJ = 2048
TM_SUB = 512
TM_MERGE = 1024
TM_FFN = 1024
FF_CHUNK = 256
TQ = 512
KEY_CHUNK = 256


def _params(*sem):
    return pltpu.CompilerParams(dimension_semantics=sem, vmem_limit_bytes=VMEM_LIMIT_BYTES)


def _const_spec(shape):
    nd = len(shape)
    return pl.BlockSpec(shape, lambda *_: (0,) * nd, pipeline_mode=pl.Buffered(1))


def _mod_spec(k, row=None):
    if row is None:
        return pl.BlockSpec((1, 1, D_MODEL), lambda b, *_: (b * N_MODS + k, 0, 0))
    return pl.BlockSpec((1, 1, D_MODEL), lambda *_: (row * N_MODS + k, 0, 0))


def _sigmoid(x):
    return 1.0 / (1.0 + jnp.exp(-x))


def _modulated_norm(x, gain, shift, scale):
    ms = jnp.mean(x * x, axis=-1, keepdims=True)
    y = x * lax.rsqrt(ms + EPS) * gain
    return y * (1.0 + scale) + shift


def _head_norm_t(zh, gain):
    ms = jnp.mean(zh * zh, axis=0, keepdims=True)
    return zh * lax.rsqrt(ms + EPS) * gain


def _rope_t(zn, cos, sin):
    half = HEAD_DIM // 2
    x1, x2 = zn[:half], zn[half:]
    return x1 * cos - x2 * sin, x2 * cos + x1 * sin


def _adaln_kernel(cond_ref, w_ref, b_ref, o_ref):
    c = cond_ref[...]
    s = (c * _sigmoid(c)).astype(BF16)
    o_ref[...] = jnp.dot(s, w_ref[...].astype(BF16), preferred_element_type=F32) + b_ref[...]


def _adaln(cond, w, b):
    n = w.shape[1]
    tn = ADALN_COLS
    assert n % tn == 0
    return pl.pallas_call(
        _adaln_kernel,
        grid=(n // tn,),
        in_specs=[_const_spec((COND_ROWS, D_MODEL)),
                  pl.BlockSpec((D_MODEL, tn), lambda j: (0, j)),
                  pl.BlockSpec((1, tn), lambda j: (0, j))],
        out_specs=pl.BlockSpec((COND_ROWS, tn), lambda j: (0, j)),
        out_shape=jax.ShapeDtypeStruct((COND_ROWS, n), F32),
        compiler_params=_params("parallel"),
        name="adaln",
    )(cond, w, b.reshape(1, n))


def _in_proj_kernel(x_ref, shift_ref, scale_ref, gnorm_ref, wqkvT_ref, wf_ref, gq_ref, gk_ref,
                    cos_ref, sin_ref, qT_ref, k_ref, vT_ref, f_ref):
    tm = x_ref.shape[1]
    gq, gk = gq_ref[...], gk_ref[...]
    half = HEAD_DIM // 2
    for j in range(tm // TM_SUB):
        tok = slice(j * TM_SUB, (j + 1) * TM_SUB)
        h = _modulated_norm(x_ref[0, tok, :], gnorm_ref[...], shift_ref[0], scale_ref[0]).astype(BF16)
        zT = lax.dot_general(wqkvT_ref[...], h, (((1,), (1,)), ((), ())), preferred_element_type=F32)
        cos, sin = cos_ref[:, tok], sin_ref[:, tok]
        for i in range(N_HEADS):
            lo = i * HEAD_DIM
            a, b = _rope_t(_head_norm_t(zT[lo:lo + HEAD_DIM], gq), cos, sin)
            qT_ref[0, lo:lo + half, tok] = a.astype(BF16)
            qT_ref[0, lo + half:lo + HEAD_DIM, tok] = b.astype(BF16)
        k_parts = []
        for i in range(N_KV_HEADS):
            lo = Q_W + i * HEAD_DIM
            k_parts.extend(_rope_t(_head_norm_t(zT[lo:lo + HEAD_DIM], gk), cos, sin))
        k_tok = jnp.concatenate(k_parts, axis=0).T
        for g in range(N_KV_HEADS):
            k_ref[0, g, tok, :] = k_tok[:, g * HEAD_DIM:(g + 1) * HEAD_DIM].astype(BF16)
        vT_ref[0, :, tok] = zT[K_END:V_END].astype(BF16)
        f_ref[0, tok, :] = jnp.dot(h, wf_ref[...], preferred_element_type=F32).astype(BF16)


def _in_proj(x, mods3, gnorm, wqkvT, wf, gq_b, gk_b, cosT, sinT, n_ctx):
    B, S, D = x.shape
    tm = TM_PROJ
    half = HEAD_DIM // 2
    return pl.pallas_call(
        _in_proj_kernel,
        grid=(B, S // tm),
        in_specs=[pl.BlockSpec((1, tm, D), lambda b, i: (b, i, 0)),
                  _mod_spec(0), _mod_spec(1),
                  _const_spec((1, D)),
                  _const_spec((V_END, D)), _const_spec((D, F_W)),
                  _const_spec((HEAD_DIM, TM_SUB)), _const_spec((HEAD_DIM, TM_SUB)),
                  pl.BlockSpec((half, tm), lambda b, i: (0, i)),
                  pl.BlockSpec((half, tm), lambda b, i: (0, i))],
        out_specs=[pl.BlockSpec((1, Q_W, tm), lambda b, i: (b, 0, i)),
                   pl.BlockSpec((1, N_KV_HEADS, tm, HEAD_DIM), lambda b, i: (b, 0, i, 0)),
                   pl.BlockSpec((1, KV_W, tm), lambda b, i: (b, 0, i)),
                   pl.BlockSpec((1, tm, F_W), lambda b, i: (b, i, 0))],
        out_shape=[jax.ShapeDtypeStruct((B, Q_W, S), BF16),
                   jax.ShapeDtypeStruct((B, N_KV_HEADS, S + n_ctx, HEAD_DIM), BF16),
                   jax.ShapeDtypeStruct((B, KV_W, S + n_ctx), BF16),
                   jax.ShapeDtypeStruct((B, S, F_W), BF16)],
        compiler_params=_params("parallel", "parallel"),
        name="in_proj",
    )(x, mods3, mods3, gnorm, wqkvT, wf, gq_b, gk_b, cosT, sinT)


def _ctx_kv_kernel(x_ref, shift_ref, scale_ref, gnorm_ref, wkvT_ref, gk_ref, k_in_ref, vT_in_ref,
                   k_ref, vT_ref):
    del k_in_ref, vT_in_ref
    gk = gk_ref[...]
    for i in range(x_ref.shape[0]):
        h = _modulated_norm(x_ref[i], gnorm_ref[...], shift_ref[0], scale_ref[0]).astype(BF16)
        zT = lax.dot_general(wkvT_ref[...], h, (((1,), (1,)), ((), ())), preferred_element_type=F32)
        kT = jnp.concatenate(
            [_head_norm_t(zT[j * HEAD_DIM:(j + 1) * HEAD_DIM], gk) for j in range(N_KV_HEADS)], axis=0)
        k_tok = kT.T
        for g in range(N_KV_HEADS):
            k_ref[i, g] = k_tok[:, g * HEAD_DIM:(g + 1) * HEAD_DIM].astype(BF16)
        vT_ref[i] = zT[KV_W:].astype(BF16)


def _ctx_kv(ctx, mods3, gnorm, wkvT, gk_b, k_all, vT_all):
    B, C, D = ctx.shape
    n_lat = k_all.shape[2] - C
    assert n_lat % C == 0 and B % CTX_BATCH == 0
    blk = n_lat // C
    nb = CTX_BATCH
    return pl.pallas_call(
        _ctx_kv_kernel,
        grid=(B // nb,),
        in_specs=[pl.BlockSpec((nb, C, D), lambda b: (b, 0, 0)),
                  _mod_spec(0, row=B), _mod_spec(1, row=B), _const_spec((1, D)),
                  _const_spec((2 * KV_W, D)), _const_spec((HEAD_DIM, C)),
                  pl.BlockSpec(memory_space=pl.ANY), pl.BlockSpec(memory_space=pl.ANY)],
        out_specs=[pl.BlockSpec((nb, N_KV_HEADS, C, HEAD_DIM), lambda b: (b, 0, blk, 0)),
                   pl.BlockSpec((nb, KV_W, C), lambda b: (b, 0, blk))],
        out_shape=[jax.ShapeDtypeStruct(k_all.shape, BF16),
                   jax.ShapeDtypeStruct(vT_all.shape, BF16)],
        input_output_aliases={6: 0, 7: 1},
        compiler_params=_params("parallel"),
        name="ctx_kv",
    )(ctx, mods3, mods3, gnorm, wkvT, gk_b, k_all, vT_all)


def _attention_kernel(qT_ref, qn_ref, k_ref, kn_ref, vT_ref, o_ref, vc_ref, s0_ref, s1_ref, m0_ref, m1_ref, oT_ref):
    first = jnp.logical_and(pl.program_id(0) == 0, pl.program_id(1) == 0)
    tq = qT_ref.shape[2]
    n_chunks = k_ref.shape[2] // KEY_CHUNK

    @pl.when(pl.program_id(1) == 0)
    def _():
        for g in range(N_KV_HEADS):
            for c in range(n_chunks):
                vc_ref[g, c] = vT_ref[0, g * HEAD_DIM:(g + 1) * HEAD_DIM, c * KEY_CHUNK:(c + 1) * KEY_CHUNK]

    def head_rows(t):
        return slice(t * HEAD_DIM, (t + 1) * HEAD_DIM)

    def step(t_b, sb, sa, a_keys=None, a_q=None):
        m8 = jnp.full((8, tq), -jnp.inf, F32)
        l8 = jnp.zeros((8, tq), F32)
        o = None
        for c in range(n_chunks):
            rows = slice(c * KEY_CHUNK, (c + 1) * KEY_CHUNK)
            srows = pl.ds(pl.multiple_of(row0 + c * KEY_CHUNK, KEY_CHUNK), KEY_CHUNK)
            if sb is not None:
                e = jnp.exp2(sb[0][srows, :] - sb[1][0:1, :])
                l8 = l8 + jnp.sum(e.reshape(KEY_CHUNK // 8, 8, tq), axis=0)
                d = jnp.dot(vc_ref[t_b // GROUP, c], e.astype(BF16), preferred_element_type=F32)
                o = d if o is None else o + d
            if sa is not None:
                s = jnp.dot(a_keys(rows), a_q(), preferred_element_type=F32)
                sa[0][srows, :] = s
                m8 = jnp.maximum(m8, jnp.max(s.reshape(KEY_CHUNK // 8, 8, tq), axis=0))
        if sa is not None:
            sa[1][...] = jnp.broadcast_to(jnp.max(m8, axis=0, keepdims=True), (8, tq))
        if sb is not None:
            oT_ref[head_rows(t_b), :] = o / jnp.sum(l8, axis=0, keepdims=True)

    def cur(t):
        return (lambda rows: k_ref[0, t // GROUP, rows, :]), (lambda: qT_ref[0, head_rows(t), :])

    bufs = ((s0_ref, m0_ref), (s1_ref, m1_ref))
    row0 = jnp.minimum(pl.program_id(0), 0)

    @pl.when(first)
    def _():
        step(-1, None, bufs[0], *cur(0))

    for t in range(N_HEADS - 1):
        step(t, bufs[t % 2], bufs[(t + 1) % 2], *cur(t + 1))
    step(N_HEADS - 1, bufs[(N_HEADS - 1) % 2], bufs[N_HEADS % 2],
         lambda rows: kn_ref[0, 0, rows, :], lambda: qn_ref[0])
    o_ref[0] = oT_ref[...].T.astype(BF16)


def _attention(qT, k_all, vT_all):
    B, _, S = qT.shape
    T = k_all.shape[2]
    assert T % KEY_CHUNK == 0 and N_HEADS % 2 == 0
    n_chunks = T // KEY_CHUNK
    nq = S // TQ

    def nxt(b, i):
        wrap = i + 1 == nq
        return jnp.where(wrap, jnp.minimum(b + 1, B - 1), b), jnp.where(wrap, 0, i + 1)

    return pl.pallas_call(
        _attention_kernel,
        grid=(B, nq),
        in_specs=[pl.BlockSpec((1, Q_W, TQ), lambda b, i: (b, 0, i)),
                  pl.BlockSpec((1, HEAD_DIM, TQ), lambda b, i: (nxt(b, i)[0], 0, nxt(b, i)[1])),
                  pl.BlockSpec((1, N_KV_HEADS, T, HEAD_DIM), lambda b, i: (b, 0, 0, 0)),
                  pl.BlockSpec((1, 1, T, HEAD_DIM), lambda b, i: (nxt(b, i)[0], 0, 0, 0)),
                  pl.BlockSpec((1, KV_W, T), lambda b, i: (b, 0, 0))],
        out_specs=pl.BlockSpec((1, TQ, Q_W), lambda b, i: (b, i, 0)),
        out_shape=jax.ShapeDtypeStruct((B, S, Q_W), BF16),
        scratch_shapes=[pltpu.VMEM((N_KV_HEADS, n_chunks, HEAD_DIM, KEY_CHUNK), BF16),
                        pltpu.VMEM((T, TQ), F32), pltpu.VMEM((T, TQ), F32),
                        pltpu.VMEM((8, TQ), F32), pltpu.VMEM((8, TQ), F32),
                        pltpu.VMEM((Q_W, TQ), F32)],
        compiler_params=_params("arbitrary", "arbitrary"),
        name="attention",
    )(qT, qT, k_all, k_all, vT_all)


POS_RADIX = 4


@functools.lru_cache(maxsize=None)
def _dft_tables(n_pos):
    def cos_sin(rows, cols, period):
        ang = 2.0 * np.pi * ((rows[:, None] * cols[None, :]) % period).astype(np.float64) / period
        return np.cos(ang), np.sin(ang)
    q = n_pos // POS_RADIX
    ch = np.arange(FGROUP_DIM, dtype=np.int64)
    cc, sc = cos_sin(ch, ch, FGROUP_DIM)
    n2 = np.arange(q, dtype=np.int64)
    cq, sq = cos_sin(n2, n2, q)
    tc, ts = cos_sin(np.arange(POS_RADIX, dtype=np.int64), n2, n_pos)
    lanes = np.ones((1, F_W))
    chan = np.concatenate([cc, sc], axis=1).astype(np.float32)
    pos = np.concatenate([cq, sq], axis=1).astype(np.float32)
    twc = (tc.reshape(n_pos, 1) * lanes).astype(np.float32)
    tws = (ts.reshape(n_pos, 1) * lanes).astype(np.float32)
    return chan, pos, twc, tws


def _fourier_kernel(f_ref, chan_ref, pos_ref, twc_ref, tws_ref, o_ref, ur_ref, ui_ref, t_ref, y_ref, *, scale):
    assert POS_RADIX == 4
    n_pos = f_ref.shape[1]
    q = n_pos // POS_RADIX
    for g in range(N_FGROUPS):
        cols = slice(g * FGROUP_DIM, (g + 1) * FGROUP_DIM)
        xcs = jnp.dot(f_ref[0, :, cols], chan_ref[...], preferred_element_type=F32)
        ur_ref[:, cols] = xcs[:, :FGROUP_DIM]
        ui_ref[:, cols] = -xcs[:, FGROUP_DIM:]
    ur = [ur_ref[i * q:(i + 1) * q, :] for i in range(POS_RADIX)]
    ui = [ui_ref[i * q:(i + 1) * q, :] for i in range(POS_RADIX)]
    er, ei = ur[0] + ur[2], ui[0] + ui[2]
    sr, si = ur[1] + ur[3], ui[1] + ui[3]
    dr, di = ur[0] - ur[2], ui[0] - ui[2]
    gr, gi = ur[1] - ur[3], ui[1] - ui[3]
    butterflies = [(er + sr, ei + si), (dr + gi, di - gr), (er - sr, ei - si), (dr - gi, di + gr)]
    for k1, (tr, ti) in enumerate(butterflies):
        if k1:
            c = twc_ref[k1 * q:(k1 + 1) * q, :]
            s = tws_ref[k1 * q:(k1 + 1) * q, :]
            tr, ti = tr * c + ti * s, ti * c - tr * s
        t_ref[k1, 0:q, :] = tr.astype(BF16)
        t_ref[k1, q:2 * q, :] = ti.astype(BF16)
    for k1 in range(POS_RADIX):
        y = jnp.dot(pos_ref[...], t_ref[k1], preferred_element_type=F32) * scale
        for g in range(N_FGROUPS):
            y_ref[g, pl.ds(k1, q, stride=POS_RADIX), :] = y[:, g * FGROUP_DIM:(g + 1) * FGROUP_DIM]
    o_ref[0] = jnp.concatenate([y_ref[g] for g in range(N_FGROUPS)], axis=1).astype(BF16)


def _fourier(f):
    B, S, W = f.shape
    assert S % POS_RADIX == 0
    q = S // POS_RADIX
    chan, pos, twc, tws = _dft_tables(S)
    scale = float(1.0 / np.sqrt(S * FGROUP_DIM))
    return pl.pallas_call(
        functools.partial(_fourier_kernel, scale=scale),
        grid=(B,),
        in_specs=[pl.BlockSpec((1, S, W), lambda b: (b, 0, 0)),
                  _const_spec((FGROUP_DIM, 2 * FGROUP_DIM)),
                  _const_spec((q, 2 * q)),
                  _const_spec((S, W)), _const_spec((S, W))],
        out_specs=pl.BlockSpec((1, S, W), lambda b: (b, 0, 0)),
        out_shape=jax.ShapeDtypeStruct((B, S, W), BF16),
        scratch_shapes=[pltpu.VMEM((S, W), F32), pltpu.VMEM((S, W), F32),
                        pltpu.VMEM((POS_RADIX, 2 * q, W), BF16),
                        pltpu.VMEM((N_FGROUPS, S, FGROUP_DIM), F32)],
        compiler_params=_params("parallel"),
        name="fourier",
    )(f, jnp.asarray(chan).astype(BF16), jnp.asarray(pos).astype(BF16), jnp.asarray(twc), jnp.asarray(tws))


def _merge_kernel(x_ref, shift_ref, scale_ref, gate_ref, gnorm_ref, wg_ref, bg_ref, attn_ref, four_ref,
                  wap_ref, wfp_ref, wo_ref, o_ref):
    x = x_ref[0]
    h = _modulated_norm(x, gnorm_ref[...], shift_ref[0], scale_ref[0]).astype(BF16)
    gates = _sigmoid(jnp.dot(h, wg_ref[...], preferred_element_type=F32) + bg_ref[...])
    a = jnp.dot(attn_ref[0], wap_ref[...], preferred_element_type=F32)
    fo = jnp.dot(four_ref[0], wfp_ref[...], preferred_element_type=F32)
    merged = gates[:, :D_MODEL] * a + gates[:, D_MODEL:] * fo
    mix = jnp.dot(merged.astype(BF16), wo_ref[...], preferred_element_type=F32)
    o_ref[0] = x + gate_ref[0] * mix


def _merge(x, mods3, gnorm, wg, bg, attn, four, wap, wfp, wo):
    B, S, D = x.shape
    tm = TM_MERGE
    return pl.pallas_call(
        _merge_kernel,
        grid=(B, S // tm),
        in_specs=[pl.BlockSpec((1, tm, D), lambda b, i: (b, i, 0)), _mod_spec(0), _mod_spec(1), _mod_spec(2),
                  _const_spec((1, D)),
                  _const_spec((D, N_BRANCH * D)), _const_spec((1, N_BRANCH * D)),
                  pl.BlockSpec((1, tm, Q_W), lambda b, i: (b, i, 0)),
                  pl.BlockSpec((1, tm, F_W), lambda b, i: (b, i, 0)),
                  _const_spec((Q_W, D)), _const_spec((F_W, D)), _const_spec((D, D))],
        out_specs=pl.BlockSpec((1, tm, D), lambda b, i: (b, i, 0)),
        out_shape=jax.ShapeDtypeStruct((B, S, D), F32),
        compiler_params=_params("parallel", "parallel"),
        name="merge",
    )(x, mods3, mods3, mods3, gnorm, wg, bg, attn, four, wap, wfp, wo)


def _ffn_kernel(x_ref, shift_ref, scale_ref, gate_ref, gnorm_ref, wgu_ref, wdown_ref, gfinal_ref, o_ref):
    x = x_ref[0]
    d_ff = wdown_ref.shape[0]
    h = _modulated_norm(x, gnorm_ref[...], shift_ref[0], scale_ref[0]).astype(BF16)
    ffn = None
    for lo in range(0, d_ff, FF_CHUNK):
        gate = jnp.dot(h, wgu_ref[:, lo:lo + FF_CHUNK], preferred_element_type=F32)
        up = jnp.dot(h, wgu_ref[:, d_ff + lo:d_ff + lo + FF_CHUNK], preferred_element_type=F32)
        act = (gate * _sigmoid(gate) * up).astype(BF16)
        part = jnp.dot(act, wdown_ref[lo:lo + FF_CHUNK, :], preferred_element_type=F32)
        ffn = part if ffn is None else ffn + part
    x2 = x + gate_ref[0] * ffn
    ms = jnp.mean(x2 * x2, axis=-1, keepdims=True)
    o_ref[0] = x2 * lax.rsqrt(ms + EPS) * gfinal_ref[...]


def _ffn(x, mods3, gnorm, wgu, wdown, gfinal):
    B, S, D = x.shape
    d_ff = wdown.shape[0]
    tm = TM_FFN
    return pl.pallas_call(
        _ffn_kernel,
        grid=(B, S // tm),
        in_specs=[pl.BlockSpec((1, tm, D), lambda b, i: (b, i, 0)), _mod_spec(3), _mod_spec(4), _mod_spec(5),
                  _const_spec((1, D)),
                  _const_spec((D, 2 * d_ff)), _const_spec((d_ff, D)), _const_spec((1, D))],
        out_specs=pl.BlockSpec((1, tm, D), lambda b, i: (b, i, 0)),
        out_shape=jax.ShapeDtypeStruct((B, S, D), F32),
        compiler_params=_params("parallel", "parallel"),
        name="ffn",
    )(x, mods3, mods3, mods3, gnorm, wgu, wdown, gfinal)


@functools.lru_cache(maxsize=None)
def _rope_tables_t(seq):
    rows = seq // GRID_W
    t_row = np.repeat(np.arange(rows, dtype=np.float64), GRID_W)
    t_col = np.tile(np.arange(GRID_W, dtype=np.float64), rows)
    inv_freq = ROPE_THETA ** (-np.arange(0, ROPE_AXIS_DIM, 2, dtype=np.float64) / ROPE_AXIS_DIM)
    ang = np.concatenate([t_row[:, None] * inv_freq, t_col[:, None] * inv_freq], axis=-1)
    return np.cos(ang).T.astype(np.float32), np.sin(ang).T.astype(np.float32)


def kernel(x, c, ctx, c_ctx, w_ada, b_ada, g_norm_mix, g_norm_ffn, w_in, b_gate, g_q, g_k, w_attn_proj,
           w_fourier_proj, w_o, w_gate_up, w_down, g_final):
    B, S, D = x.shape
    C = ctx.shape[1]
    assert D == D_MODEL and w_in.shape[0] == 1 and B < COND_ROWS
    assert S % TM_PROJ == 0 and TM_PROJ % TM_SUB == 0 and S % TM_MERGE == 0 and S % TM_FFN == 0 and S % TQ == 0
    assert w_down.shape[1] % FF_CHUNK == 0
    layer = 0

    cond = jnp.zeros((COND_ROWS, D), F32).at[:B].set(c).at[B].set(c_ctx)
    mods = _adaln(cond, w_ada[layer], b_ada[layer])
    mods3 = mods.reshape(COND_ROWS * N_MODS, 1, D)

    w_in_l = w_in[layer]
    wqkvT = w_in_l[:, :V_END].T.astype(BF16)
    wkvT = w_in_l[:, Q_W:V_END].T.astype(BF16)
    wf = w_in_l[:, V_END:F_END].astype(BF16)
    wg = w_in_l[:, F_END:].astype(BF16)
    gnorm_mix = g_norm_mix[layer].reshape(1, D)
    gnorm_ffn = g_norm_ffn[layer].reshape(1, D)

    gq_col = (g_q[layer] * (HEAD_DIM ** -0.5 * LOG2_E)).reshape(HEAD_DIM, 1)
    gk_col = g_k[layer].reshape(HEAD_DIM, 1)
    gq_b = jnp.broadcast_to(gq_col, (HEAD_DIM, TM_SUB))
    gk_b = jnp.broadcast_to(gk_col, (HEAD_DIM, TM_SUB))
    gk_ctx = jnp.broadcast_to(gk_col, (HEAD_DIM, C))
    cosT, sinT = (jnp.asarray(t) for t in _rope_tables_t(S))

    qT, k_all, vT_all, f = _in_proj(x, mods3, gnorm_mix, wqkvT, wf, gq_b, gk_b, cosT, sinT, C)
    k_all, vT_all = _ctx_kv(ctx, mods3, gnorm_mix, wkvT, gk_ctx, k_all, vT_all)
    attn = _attention(qT, k_all, vT_all)
    four = _fourier(f)
    x1 = _merge(x, mods3, gnorm_mix, wg, b_gate[layer].reshape(1, -1), attn, four,
                w_attn_proj[layer].astype(BF16), w_fourier_proj[layer].astype(BF16), w_o[layer].astype(BF16))
    return _ffn(x1, mods3, gnorm_ffn, w_gate_up[layer].astype(BF16), w_down[layer].astype(BF16),
                g_final.reshape(1, D))
```

```python
import functools

import numpy as np
import jax
import jax.numpy as jnp
from jax import lax
from jax.experimental import pallas as pl
from jax.experimental.pallas import tpu as pltpu

D_MODEL = 1024
GRID_W = 64
HEAD_DIM = 64
N_HEADS = 16
N_KV_HEADS = 4
GROUP = N_HEADS // N_KV_HEADS
ROPE_AXIS_DIM = HEAD_DIM // 2
ROPE_THETA = 10000.0
N_FGROUPS = 4
FGROUP_DIM = 128
N_BRANCH = 2
Q_W = N_HEADS * HEAD_DIM
KV_W = N_KV_HEADS * HEAD_DIM
F_W = N_FGROUPS * FGROUP_DIM
K_END = Q_W + KV_W
V_END = K_END + KV_W
F_END = V_END + F_W
EPS = 1e-6
LOG2_E = 1.4426950408889634

BF16 = jnp.bfloat16
F32 = jnp.float32

VMEM_LIMIT_BYTES = 56 * 1024 * 1024

COND_ROWS = 16
N_MODS = 6
TM_PROJ = 2048
TM_SUB = 512
TM_MERGE = 1024
TM_FFN = 1024
FF_CHUNK = 256
TQ = 512
KEY_CHUNK = 256


def _params(*sem):
    return pltpu.CompilerParams(dimension_semantics=sem, vmem_limit_bytes=VMEM_LIMIT_BYTES)


def _const_spec(shape):
    nd = len(shape)
    return pl.BlockSpec(shape, lambda *_: (0,) * nd, pipeline_mode=pl.Buffered(1))


def _mod_spec(k, row=None):
    if row is None:
        return pl.BlockSpec((1, 1, D_MODEL), lambda b, *_: (b * N_MODS + k, 0, 0))
    return pl.BlockSpec((1, 1, D_MODEL), lambda *_: (row * N_MODS + k, 0, 0))


def _sigmoid(x):
    return 1.0 / (1.0 + jnp.exp(-x))


def _modulated_norm(x, gain, shift, scale):
    ms = jnp.mean(x * x, axis=-1, keepdims=True)
    y = x * lax.rsqrt(ms + EPS) * gain
    return y * (1.0 + scale) + shift


def _head_norm_t(zh, gain):
    ms = jnp.mean(zh * zh, axis=0, keepdims=True)
    return zh * lax.rsqrt(ms + EPS) * gain


def _rope_t(zn, cos, sin):
    half = HEAD_DIM // 2
    x1, x2 = zn[:half], zn[half:]
    return x1 * cos - x2 * sin, x2 * cos + x1 * sin


def _adaln_kernel(cond_ref, w_ref, b_ref, o_ref):
    c = cond_ref[...]
    s = c * _sigmoid(c)
    o_ref[...] = jnp.dot(s, w_ref[...], preferred_element_type=F32,
                         precision=lax.Precision.HIGHEST) + b_ref[...]


def _adaln(cond, w, b):
    n = w.shape[1]
    tn = n // 4
    return pl.pallas_call(
        _adaln_kernel,
        grid=(n // tn,),
        in_specs=[_const_spec((COND_ROWS, D_MODEL)),
                  pl.BlockSpec((D_MODEL, tn), lambda j: (0, j)),
                  pl.BlockSpec((1, tn), lambda j: (0, j))],
        out_specs=pl.BlockSpec((COND_ROWS, tn), lambda j: (0, j)),
        out_shape=jax.ShapeDtypeStruct((COND_ROWS, n), F32),
        compiler_params=_params("parallel"),
        name="adaln",
    )(cond, w, b.reshape(1, n))


def _in_proj_kernel(x_ref, shift_ref, scale_ref, gnorm_ref, wqkvT_ref, wf_ref, gq_ref, gk_ref,
                    cos_ref, sin_ref, qT_ref, k_ref, vT_ref, f_ref):
    tm = x_ref.shape[1]
    gq, gk = gq_ref[...], gk_ref[...]
    half = HEAD_DIM // 2
    for j in range(tm // TM_SUB):
        tok = slice(j * TM_SUB, (j + 1) * TM_SUB)
        h = _modulated_norm(x_ref[0, tok, :], gnorm_ref[...], shift_ref[0], scale_ref[0]).astype(BF16)
        zT = lax.dot_general(wqkvT_ref[...], h, (((1,), (1,)), ((), ())), preferred_element_type=F32)
        cos, sin = cos_ref[:, tok], sin_ref[:, tok]
        for i in range(N_HEADS):
            lo = i * HEAD_DIM
            a, b = _rope_t(_head_norm_t(zT[lo:lo + HEAD_DIM], gq), cos, sin)
            qT_ref[0, lo:lo + half, tok] = a.astype(BF16)
            qT_ref[0, lo + half:lo + HEAD_DIM, tok] = b.astype(BF16)
        k_parts = []
        for i in range(N_KV_HEADS):
            lo = Q_W + i * HEAD_DIM
            k_parts.extend(_rope_t(_head_norm_t(zT[lo:lo + HEAD_DIM], gk), cos, sin))
        k_tok = jnp.concatenate(k_parts, axis=0).T
        for g in range(N_KV_HEADS):
            k_ref[0, g, tok, :] = k_tok[:, g * HEAD_DIM:(g + 1) * HEAD_DIM].astype(BF16)
        vT_ref[0, :, tok] = zT[K_END:V_END].astype(BF16)
        f_ref[0, tok, :] = jnp.dot(h, wf_ref[...], preferred_element_type=F32).astype(BF16)


def _in_proj(x, mods3, gnorm, wqkvT, wf, gq_b, gk_b, cosT, sinT, n_ctx):
    B, S, D = x.shape
    tm = TM_PROJ
    half = HEAD_DIM // 2
    return pl.pallas_call(
        _in_proj_kernel,
        grid=(B, S // tm),
        in_specs=[pl.BlockSpec((1, tm, D), lambda b, i: (b, i, 0)),
                  _mod_spec(0), _mod_spec(1),
                  _const_spec((1, D)),
                  _const_spec((V_END, D)), _const_spec((D, F_W)),
                  _const_spec((HEAD_DIM, TM_SUB)), _const_spec((HEAD_DIM, TM_SUB)),
                  pl.BlockSpec((half, tm), lambda b, i: (0, i)),
                  pl.BlockSpec((half, tm), lambda b, i: (0, i))],
        out_specs=[pl.BlockSpec((1, Q_W, tm), lambda b, i: (b, 0, i)),
                   pl.BlockSpec((1, N_KV_HEADS, tm, HEAD_DIM), lambda b, i: (b, 0, i, 0)),
                   pl.BlockSpec((1, KV_W, tm), lambda b, i: (b, 0, i)),
                   pl.BlockSpec((1, tm, F_W), lambda b, i: (b, i, 0))],
        out_shape=[jax.ShapeDtypeStruct((B, Q_W, S), BF16),
                   jax.ShapeDtypeStruct((B, N_KV_HEADS, S + n_ctx, HEAD_DIM), BF16),
                   jax.ShapeDtypeStruct((B, KV_W, S + n_ctx), BF16),
                   jax.ShapeDtypeStruct((B, S, F_W), BF16)],
        compiler_params=_params("parallel", "parallel"),
        name="in_proj",
    )(x, mods3, mods3, gnorm, wqkvT, wf, gq_b, gk_b, cosT, sinT)


def _ctx_kv_kernel(x_ref, shift_ref, scale_ref, gnorm_ref, wkvT_ref, gk_ref, k_in_ref, vT_in_ref,
                   k_ref, vT_ref):
    del k_in_ref, vT_in_ref
    h = _modulated_norm(x_ref[0], gnorm_ref[...], shift_ref[0], scale_ref[0]).astype(BF16)
    zT = lax.dot_general(wkvT_ref[...], h, (((1,), (1,)), ((), ())), preferred_element_type=F32)
    gk = gk_ref[...]
    kT = jnp.concatenate(
        [_head_norm_t(zT[i * HEAD_DIM:(i + 1) * HEAD_DIM], gk) for i in range(N_KV_HEADS)], axis=0)
    k_tok = kT.T
    for g in range(N_KV_HEADS):
        k_ref[0, g] = k_tok[:, g * HEAD_DIM:(g + 1) * HEAD_DIM].astype(BF16)
    vT_ref[0] = zT[KV_W:].astype(BF16)


def _ctx_kv(ctx, mods3, gnorm, wkvT, gk_b, k_all, vT_all):
    B, C, D = ctx.shape
    n_lat = k_all.shape[2] - C
    assert n_lat % C == 0
    blk = n_lat // C
    return pl.pallas_call(
        _ctx_kv_kernel,
        grid=(B,),
        in_specs=[pl.BlockSpec((1, C, D), lambda b: (b, 0, 0)),
                  _mod_spec(0, row=B), _mod_spec(1, row=B), _const_spec((1, D)),
                  _const_spec((2 * KV_W, D)), _const_spec((HEAD_DIM, C)),
                  pl.BlockSpec(memory_space=pl.ANY), pl.BlockSpec(memory_space=pl.ANY)],
        out_specs=[pl.BlockSpec((1, N_KV_HEADS, C, HEAD_DIM), lambda b: (b, 0, blk, 0)),
                   pl.BlockSpec((1, KV_W, C), lambda b: (b, 0, blk))],
        out_shape=[jax.ShapeDtypeStruct(k_all.shape, BF16),
                   jax.ShapeDtypeStruct(vT_all.shape, BF16)],
        input_output_aliases={6: 0, 7: 1},
        compiler_params=_params("parallel"),
        name="ctx_kv",
    )(ctx, mods3, mods3, gnorm, wkvT, gk_b, k_all, vT_all)


def _attention_kernel(qT_ref, qn_ref, k_ref, kn_ref, vT_ref, o_ref, vc_ref, s0_ref, s1_ref, m0_ref, m1_ref, oT_ref):
    first = jnp.logical_and(pl.program_id(0) == 0, pl.program_id(1) == 0)
    tq = qT_ref.shape[2]
    n_chunks = k_ref.shape[2] // KEY_CHUNK

    @pl.when(pl.program_id(1) == 0)
    def _():
        for g in range(N_KV_HEADS):
            for c in range(n_chunks):
                vc_ref[g, c] = vT_ref[0, g * HEAD_DIM:(g + 1) * HEAD_DIM, c * KEY_CHUNK:(c + 1) * KEY_CHUNK]

    def head_rows(t):
        return slice(t * HEAD_DIM, (t + 1) * HEAD_DIM)

    def step(t_b, sb, sa, a_keys=None, a_q=None):
        m8 = jnp.full((8, tq), -jnp.inf, F32)
        l8 = jnp.zeros((8, tq), F32)
        o = None
        for c in range(n_chunks):
            rows = slice(c * KEY_CHUNK, (c + 1) * KEY_CHUNK)
            srows = pl.ds(pl.multiple_of(row0 + c * KEY_CHUNK, KEY_CHUNK), KEY_CHUNK)
            if sb is not None:
                e = jnp.exp2(sb[0][srows, :] - sb[1][0:1, :])
                l8 = l8 + jnp.sum(e.reshape(KEY_CHUNK // 8, 8, tq), axis=0)
                d = jnp.dot(vc_ref[t_b // GROUP, c], e.astype(BF16), preferred_element_type=F32)
                o = d if o is None else o + d
            if sa is not None:
                s = jnp.dot(a_keys(rows), a_q(), preferred_element_type=F32)
                sa[0][srows, :] = s
                m8 = jnp.maximum(m8, jnp.max(s.reshape(KEY_CHUNK // 8, 8, tq), axis=0))
        if sa is not None:
            sa[1][...] = jnp.broadcast_to(jnp.max(m8, axis=0, keepdims=True), (8, tq))
        if sb is not None:
            oT_ref[head_rows(t_b), :] = o / jnp.sum(l8, axis=0, keepdims=True)

    def cur(t):
        return (lambda rows: k_ref[0, t // GROUP, rows, :]), (lambda: qT_ref[0, head_rows(t), :])

    bufs = ((s0_ref, m0_ref), (s1_ref, m1_ref))
    row0 = jnp.minimum(pl.program_id(0), 0)

    @pl.when(first)
    def _():
        step(-1, None, bufs[0], *cur(0))

    for t in range(N_HEADS - 1):
        step(t, bufs[t % 2], bufs[(t + 1) % 2], *cur(t + 1))
    step(N_HEADS - 1, bufs[(N_HEADS - 1) % 2], bufs[N_HEADS % 2],
         lambda rows: kn_ref[0, 0, rows, :], lambda: qn_ref[0])
    o_ref[0] = oT_ref[...].T.astype(BF16)


def _attention(qT, k_all, vT_all):
    B, _, S = qT.shape
    T = k_all.shape[2]
    assert T % KEY_CHUNK == 0 and N_HEADS % 2 == 0
    n_chunks = T // KEY_CHUNK
    nq = S // TQ

    def nxt(b, i):
        wrap = i + 1 == nq
        return jnp.where(wrap, jnp.minimum(b + 1, B - 1), b), jnp.where(wrap, 0, i + 1)

    return pl.pallas_call(
        _attention_kernel,
        grid=(B, nq),
        in_specs=[pl.BlockSpec((1, Q_W, TQ), lambda b, i: (b, 0, i)),
                  pl.BlockSpec((1, HEAD_DIM, TQ), lambda b, i: (nxt(b, i)[0], 0, nxt(b, i)[1])),
                  pl.BlockSpec((1, N_KV_HEADS, T, HEAD_DIM), lambda b, i: (b, 0, 0, 0)),
                  pl.BlockSpec((1, 1, T, HEAD_DIM), lambda b, i: (nxt(b, i)[0], 0, 0, 0)),
                  pl.BlockSpec((1, KV_W, T), lambda b, i: (b, 0, 0))],
        out_specs=pl.BlockSpec((1, TQ, Q_W), lambda b, i: (b, i, 0)),
        out_shape=jax.ShapeDtypeStruct((B, S, Q_W), BF16),
        scratch_shapes=[pltpu.VMEM((N_KV_HEADS, n_chunks, HEAD_DIM, KEY_CHUNK), BF16),
                        pltpu.VMEM((T, TQ), F32), pltpu.VMEM((T, TQ), F32),
                        pltpu.VMEM((8, TQ), F32), pltpu.VMEM((8, TQ), F32),
                        pltpu.VMEM((Q_W, TQ), F32)],
        compiler_params=_params("arbitrary", "arbitrary"),
        name="attention",
    )(qT, qT, k_all, k_all, vT_all)


POS_RADIX = 4


@functools.lru_cache(maxsize=None)
def _dft_tables(n_pos):
    def cos_sin(rows, cols, period):
        ang = 2.0 * np.pi * ((rows[:, None] * cols[None, :]) % period).astype(np.float64) / period
        return np.cos(ang), np.sin(ang)
    q = n_pos // POS_RADIX
    ch = np.arange(FGROUP_DIM, dtype=np.int64)
    cc, sc = cos_sin(ch, ch, FGROUP_DIM)
    n2 = np.arange(q, dtype=np.int64)
    cq, sq = cos_sin(n2, n2, q)
    tc, ts = cos_sin(np.arange(POS_RADIX, dtype=np.int64), n2, n_pos)
    lanes = np.ones((1, F_W))
    chan = np.concatenate([cc, sc], axis=1).astype(np.float32)
    pos = np.concatenate([cq, sq], axis=1).astype(np.float32)
    twc = (tc.reshape(n_pos, 1) * lanes).astype(np.float32)
    tws = (ts.reshape(n_pos, 1) * lanes).astype(np.float32)
    return chan, pos, twc, tws


def _fourier_kernel(f_ref, chan_ref, pos_ref, twc_ref, tws_ref, o_ref, ur_ref, ui_ref, t_ref, y_ref, *, scale):
    assert POS_RADIX == 4
    n_pos = f_ref.shape[1]
    q = n_pos // POS_RADIX
    for g in range(N_FGROUPS):
        cols = slice(g * FGROUP_DIM, (g + 1) * FGROUP_DIM)
        xcs = jnp.dot(f_ref[0, :, cols], chan_ref[...], preferred_element_type=F32)
        ur_ref[:, cols] = xcs[:, :FGROUP_DIM]
        ui_ref[:, cols] = -xcs[:, FGROUP_DIM:]
    ur = [ur_ref[i * q:(i + 1) * q, :] for i in range(POS_RADIX)]
    ui = [ui_ref[i * q:(i + 1) * q, :] for i in range(POS_RADIX)]
    er, ei = ur[0] + ur[2], ui[0] + ui[2]
    sr, si = ur[1] + ur[3], ui[1] + ui[3]
    dr, di = ur[0] - ur[2], ui[0] - ui[2]
    gr, gi = ur[1] - ur[3], ui[1] - ui[3]
    butterflies = [(er + sr, ei + si), (dr + gi, di - gr), (er - sr, ei - si), (dr - gi, di + gr)]
    for k1, (tr, ti) in enumerate(butterflies):
        if k1:
            c = twc_ref[k1 * q:(k1 + 1) * q, :]
            s = tws_ref[k1 * q:(k1 + 1) * q, :]
            tr, ti = tr * c + ti * s, ti * c - tr * s
        t_ref[k1, 0:q, :] = tr.astype(BF16)
        t_ref[k1, q:2 * q, :] = ti.astype(BF16)
    for k1 in range(POS_RADIX):
        y = jnp.dot(pos_ref[...], t_ref[k1], preferred_element_type=F32) * scale
        for g in range(N_FGROUPS):
            y_ref[g, pl.ds(k1, q, stride=POS_RADIX), :] = y[:, g * FGROUP_DIM:(g + 1) * FGROUP_DIM]
    o_ref[0] = jnp.concatenate([y_ref[g] for g in range(N_FGROUPS)], axis=1).astype(BF16)


def _fourier(f):
    B, S, W = f.shape
    assert S % POS_RADIX == 0
    q = S // POS_RADIX
    chan, pos, twc, tws = _dft_tables(S)
    scale = float(1.0 / np.sqrt(S * FGROUP_DIM))
    return pl.pallas_call(
        functools.partial(_fourier_kernel, scale=scale),
        grid=(B,),
        in_specs=[pl.BlockSpec((1, S, W), lambda b: (b, 0, 0)),
                  _const_spec((FGROUP_DIM, 2 * FGROUP_DIM)),
                  _const_spec((q, 2 * q)),
                  _const_spec((S, W)), _const_spec((S, W))],
        out_specs=pl.BlockSpec((1, S, W), lambda b: (b, 0, 0)),
        out_shape=jax.ShapeDtypeStruct((B, S, W), BF16),
        scratch_shapes=[pltpu.VMEM((S, W), F32), pltpu.VMEM((S, W), F32),
                        pltpu.VMEM((POS_RADIX, 2 * q, W), BF16),
                        pltpu.VMEM((N_FGROUPS, S, FGROUP_DIM), F32)],
        compiler_params=_params("parallel"),
        name="fourier",
    )(f, jnp.asarray(chan).astype(BF16), jnp.asarray(pos).astype(BF16), jnp.asarray(twc), jnp.asarray(tws))


def _merge_kernel(x_ref, shift_ref, scale_ref, gate_ref, gnorm_ref, wg_ref, bg_ref, attn_ref, four_ref,
                  wap_ref, wfp_ref, wo_ref, o_ref):
    x = x_ref[0]
    h = _modulated_norm(x, gnorm_ref[...], shift_ref[0], scale_ref[0]).astype(BF16)
    gates = _sigmoid(jnp.dot(h, wg_ref[...], preferred_element_type=F32) + bg_ref[...])
    a = jnp.dot(attn_ref[0], wap_ref[...], preferred_element_type=F32)
    fo = jnp.dot(four_ref[0], wfp_ref[...], preferred_element_type=F32)
    merged = gates[:, :D_MODEL] * a + gates[:, D_MODEL:] * fo
    mix = jnp.dot(merged.astype(BF16), wo_ref[...], preferred_element_type=F32)
    o_ref[0] = x + gate_ref[0] * mix


def _merge(x, mods3, gnorm, wg, bg, attn, four, wap, wfp, wo):
    B, S, D = x.shape
    tm = TM_MERGE
    return pl.pallas_call(
        _merge_kernel,
        grid=(B, S // tm),
        in_specs=[pl.BlockSpec((1, tm, D), lambda b, i: (b, i, 0)), _mod_spec(0), _mod_spec(1), _mod_spec(2),
                  _const_spec((1, D)),
                  _const_spec((D, N_BRANCH * D)), _const_spec((1, N_BRANCH * D)),
                  pl.BlockSpec((1, tm, Q_W), lambda b, i: (b, i, 0)),
                  pl.BlockSpec((1, tm, F_W), lambda b, i: (b, i, 0)),
                  _const_spec((Q_W, D)), _const_spec((F_W, D)), _const_spec((D, D))],
        out_specs=pl.BlockSpec((1, tm, D), lambda b, i: (b, i, 0)),
        out_shape=jax.ShapeDtypeStruct((B, S, D), F32),
        compiler_params=_params("parallel", "parallel"),
        name="merge",
    )(x, mods3, mods3, mods3, gnorm, wg, bg, attn, four, wap, wfp, wo)


def _ffn_kernel(x_ref, shift_ref, scale_ref, gate_ref, gnorm_ref, wgu_ref, wdown_ref, gfinal_ref, o_ref):
    x = x_ref[0]
    d_ff = wdown_ref.shape[0]
    h = _modulated_norm(x, gnorm_ref[...], shift_ref[0], scale_ref[0]).astype(BF16)
    ffn = None
    for lo in range(0, d_ff, FF_CHUNK):
        gate = jnp.dot(h, wgu_ref[:, lo:lo + FF_CHUNK], preferred_element_type=F32)
        up = jnp.dot(h, wgu_ref[:, d_ff + lo:d_ff + lo + FF_CHUNK], preferred_element_type=F32)
        act = (gate * _sigmoid(gate) * up).astype(BF16)
        part = jnp.dot(act, wdown_ref[lo:lo + FF_CHUNK, :], preferred_element_type=F32)
        ffn = part if ffn is None else ffn + part
    x2 = x + gate_ref[0] * ffn
    ms = jnp.mean(x2 * x2, axis=-1, keepdims=True)
    o_ref[0] = x2 * lax.rsqrt(ms + EPS) * gfinal_ref[...]


def _ffn(x, mods3, gnorm, wgu, wdown, gfinal):
    B, S, D = x.shape
    d_ff = wdown.shape[0]
    tm = TM_FFN
    return pl.pallas_call(
        _ffn_kernel,
        grid=(B, S // tm),
        in_specs=[pl.BlockSpec((1, tm, D), lambda b, i: (b, i, 0)), _mod_spec(3), _mod_spec(4), _mod_spec(5),
                  _const_spec((1, D)),
                  _const_spec((D, 2 * d_ff)), _const_spec((d_ff, D)), _const_spec((1, D))],
        out_specs=pl.BlockSpec((1, tm, D), lambda b, i: (b, i, 0)),
        out_shape=jax.ShapeDtypeStruct((B, S, D), F32),
        compiler_params=_params("parallel", "parallel"),
        name="ffn",
    )(x, mods3, mods3, mods3, gnorm, wgu, wdown, gfinal)


@functools.lru_cache(maxsize=None)
def _rope_tables_t(seq):
    rows = seq // GRID_W
    t_row = np.repeat(np.arange(rows, dtype=np.float64), GRID_W)
    t_col = np.tile(np.arange(GRID_W, dtype=np.float64), rows)
    inv_freq = ROPE_THETA ** (-np.arange(0, ROPE_AXIS_DIM, 2, dtype=np.float64) / ROPE_AXIS_DIM)
    ang = np.concatenate([t_row[:, None] * inv_freq, t_col[:, None] * inv_freq], axis=-1)
    return np.cos(ang).T.astype(np.float32), np.sin(ang).T.astype(np.float32)


def kernel(x, c, ctx, c_ctx, w_ada, b_ada, g_norm_mix, g_norm_ffn, w_in, b_gate, g_q, g_k, w_attn_proj,
           w_fourier_proj, w_o, w_gate_up, w_down, g_final):
    B, S, D = x.shape
    C = ctx.shape[1]
    assert D == D_MODEL and w_in.shape[0] == 1 and B < COND_ROWS
    assert S % TM_PROJ == 0 and TM_PROJ % TM_SUB == 0 and S % TM_MERGE == 0 and S % TM_FFN == 0 and S % TQ == 0
    assert w_down.shape[1] % FF_CHUNK == 0
    layer = 0

    cond = jnp.zeros((COND_ROWS, D), F32).at[:B].set(c).at[B].set(c_ctx)
    mods = _adaln(cond, w_ada[layer], b_ada[layer])
    mods3 = mods.reshape(COND_ROWS * N_MODS, 1, D)

    w_in_l = w_in[layer]
    wqkvT = w_in_l[:, :V_END].T.astype(BF16)
    wkvT = w_in_l[:, Q_W:V_END].T.astype(BF16)
    wf = w_in_l[:, V_END:F_END].astype(BF16)
    wg = w_in_l[:, F_END:].astype(BF16)
    gnorm_mix = g_norm_mix[layer].reshape(1, D)
    gnorm_ffn = g_norm_ffn[layer].reshape(1, D)

    gq_col = (g_q[layer] * (HEAD_DIM ** -0.5 * LOG2_E)).reshape(HEAD_DIM, 1)
    gk_col = g_k[layer].reshape(HEAD_DIM, 1)
    gq_b = jnp.broadcast_to(gq_col, (HEAD_DIM, TM_SUB))
    gk_b = jnp.broadcast_to(gk_col, (HEAD_DIM, TM_SUB))
    gk_ctx = jnp.broadcast_to(gk_col, (HEAD_DIM, C))
    cosT, sinT = (jnp.asarray(t) for t in _rope_tables_t(S))

    qT, k_all, vT_all, f = _in_proj(x, mods3, gnorm_mix, wqkvT, wf, gq_b, gk_b, cosT, sinT, C)
    k_all, vT_all = _ctx_kv(ctx, mods3, gnorm_mix, wkvT, gk_ctx, k_all, vT_all)
    attn = _attention(qT, k_all, vT_all)
    four = _fourier(f)
    x1 = _merge(x, mods3, gnorm_mix, wg, b_gate[layer].reshape(1, -1), attn, four,
                w_attn_proj[layer].astype(BF16), w_fourier_proj[layer].astype(BF16), w_o[layer].astype(BF16))
    return _ffn(x1, mods3, gnorm_ffn, w_gate_up[layer].astype(BF16), w_down[layer].astype(BF16),
                g_final.reshape(1, D))
```
